```python
import jax, jax.numpy as jnp
from jax import lax
import numpy as np

D_MODEL = 1024
BATCH = 8
SEQ = 8192
DEPTH = 4

D_MIX = D_MODEL
MLA_HEADS = 4
MLA_NOPE = 128
MLA_ROPE = 64
MLA_V = 128
MLA_Q_RANK = 256
MLA_KV_RANK = 128
MLA_WIDTH = MLA_HEADS * MLA_V
ROPE_BASE = 10000.0
Q_BLOCK = 128
RWKV_WIDTH = D_MIX - MLA_WIDTH
RWKV_HEAD = 64
RWKV_HEADS = RWKV_WIDTH // RWKV_HEAD
DECAY_LORA = 64
AAA_LORA = 64
GATE_LORA = 128
GN_EPS = 64e-5
D_FF = 2816
CONV_W = 3
EPS = 1e-6
MLA_SIZES = (MLA_Q_RANK, MLA_KV_RANK, MLA_ROPE)
RWKV_SIZES = (RWKV_WIDTH, RWKV_WIDTH, RWKV_WIDTH, DECAY_LORA, DECAY_LORA, AAA_LORA, AAA_LORA, GATE_LORA)
MLA_IN = MLA_Q_RANK + MLA_KV_RANK + MLA_ROPE
RWKV_IN = 3 * RWKV_WIDTH + 2 * DECAY_LORA + 2 * AAA_LORA + GATE_LORA
D_IN = MLA_IN + RWKV_IN

kernel_name = 'hybrid_mla_rwkv7_convffn_adaln_encoder'


def rms_norm(x, gain=None, eps=EPS):
    xf = x.astype(jnp.float32)
    y = xf * lax.rsqrt(jnp.mean(xf * xf, axis=-1, keepdims=True) + eps)
    if gain is not None:
        y = y * gain.astype(jnp.float32)
    return y.astype(x.dtype)


def _split(z, sizes):
    offs = []
    acc = 0
    for s in sizes[:-1]:
        acc += s
        offs.append(acc)
    return jnp.split(z, offs, axis=-1)


def shift_prev(z):
    return jnp.pad(z, ((0, 0), (1, 0), (0, 0)))[:, :-1]


def shift_next(z):
    return jnp.pad(z, ((0, 0), (0, 1), (0, 0)))[:, 1:]


def centred_token_shift(z, mu):
    return z + mu[0] * (shift_prev(z) - z) + mu[1] * (shift_next(z) - z)


def rope_tables(pos):
    half = MLA_ROPE // 2
    inv_freq = ROPE_BASE ** (-jnp.arange(half, dtype=jnp.float32) / half)
    ang = pos.astype(jnp.float32)[..., None] * inv_freq
    return jnp.cos(ang), jnp.sin(ang)


def apply_rope(x, cos, sin):
    x1, x2 = jnp.split(x, 2, axis=-1)
    cos = cos.astype(x.dtype)
    sin = sin.astype(x.dtype)
    return jnp.concatenate([x1 * cos - x2 * sin, x2 * cos + x1 * sin], axis=-1)


def mla_group(q_lat, kv_lat, k_pe, positions, q_norm, w_uq, kv_norm, w_ukv, out_norm):
    B, S, _ = q_lat.shape
    H = MLA_HEADS
    q = (rms_norm(q_lat, q_norm) @ w_uq).reshape(B, S, H, MLA_NOPE + MLA_ROPE)
    q_nope, q_pe = q[..., :MLA_NOPE], q[..., MLA_NOPE:]
    kv = (rms_norm(kv_lat, kv_norm) @ w_ukv).reshape(B, S, H, MLA_NOPE + MLA_V)
    k_nope, v = kv[..., :MLA_NOPE], kv[..., MLA_NOPE:]
    cos, sin = rope_tables(positions)
    q_pe = apply_rope(q_pe, cos[:, :, None, :], sin[:, :, None, :])
    k_pe = apply_rope(k_pe, cos, sin)
    scale = (MLA_NOPE + MLA_ROPE) ** -0.5
    n_blk = S // Q_BLOCK
    qn_b = (q_nope * scale).reshape(B, n_blk, Q_BLOCK, H, MLA_NOPE).transpose(1, 0, 2, 3, 4)
    qp_b = (q_pe * scale).reshape(B, n_blk, Q_BLOCK, H, MLA_ROPE).transpose(1, 0, 2, 3, 4)

    def attend(blk):
        qn, qp = blk
        s = jnp.einsum('bqhd,bkhd->bhqk', qn, k_nope) + jnp.einsum('bqhr,bkr->bhqk', qp, k_pe)
        p = jax.nn.softmax(s.astype(jnp.float32), axis=-1).astype(v.dtype)
        return jnp.einsum('bhqk,bkhd->bqhd', p, v)

    o = lax.map(attend, (qn_b, qp_b))
    o = o.transpose(1, 0, 2, 3, 4).reshape(B, S, H * MLA_V)
    return rms_norm(o, out_norm)


def wkv7_scan(r, w, k, v, kk, a, reverse):
    B, S, H, N = r.shape
    xs = tuple(t.astype(jnp.float32).transpose(1, 0, 2, 3) for t in (r, w, k, v, kk, a))

    def step(state, inp):
        r_t, w_t, k_t, v_t, kk_t, a_t = inp
        sa = jnp.einsum('bhvk,bhk->bhv', state, -kk_t)
        state = (state * w_t[:, :, None, :]
                 + sa[..., None] * (kk_t * a_t)[:, :, None, :]
                 + v_t[..., None] * k_t[:, :, None, :])
        y = jnp.einsum('bhvk,bhk->bhv', state, r_t)
        return state, y

    state0 = jnp.zeros((B, H, N, N), jnp.float32)
    _, ys = lax.scan(step, state0, xs, reverse=reverse)
    return ys.transpose(1, 0, 2, 3)


def rwkv_decay(xw, w0, w2):
    wl = (w0 + jnp.tanh(xw) @ w2).astype(jnp.float32)
    wl = -jax.nn.softplus(-wl) - 0.5
    return jnp.exp(-jnp.exp(wl))


def rwkv7_group(r, k, v, xw_f, xw_b, xa_f, xa_b, xg, w0, w2, a0, a2, g2, k_k, k_a, r_k, gn_g, gn_b):
    B, S, W = r.shape
    H, N = RWKV_HEADS, RWKV_HEAD

    def heads(t):
        return t.reshape(B, S, H, N)

    g = jax.nn.sigmoid(xg) @ g2
    kk = heads(k * k_k).astype(jnp.float32)
    kk = kk * lax.rsqrt(jnp.sum(kk * kk, axis=-1, keepdims=True) + 1e-12)

    def direction(xw, xa, d, reverse):
        w = rwkv_decay(xw, w0[d], w2[d])
        a = jax.nn.sigmoid(a0[d] + xa @ a2[d])
        k_d = k * (1 + (a - 1) * k_a)
        y = wkv7_scan(heads(r), heads(w), heads(k_d), heads(v), kk, heads(a), reverse)
        return y, k_d

    y_f, k_f = direction(xw_f, xa_f, 0, False)
    y_b, k_b = direction(xw_b, xa_b, 1, True)
    y = y_f + y_b
    mu = jnp.mean(y, axis=-1, keepdims=True)
    var = jnp.mean(jnp.square(y - mu), axis=-1, keepdims=True)
    y = ((y - mu) * lax.rsqrt(var + GN_EPS)).reshape(B, S, W) * gn_g + gn_b
    k_bonus = 0.5 * (k_f + k_b)
    bonus = jnp.sum(heads(r * k_bonus) * r_k, axis=-1, keepdims=True) * heads(v)
    return (y.astype(r.dtype) + bonus.reshape(B, S, W)) * g


def conv_ffn(h, w_up, conv_w, conv_b, w_down):
    u = h @ w_up
    gate, val = u[..., :D_FF], u[..., D_FF:]
    gate = conv_w[0] * shift_prev(gate) + conv_w[1] * gate + conv_w[2] * shift_next(gate) + conv_b
    return (jax.nn.silu(gate) * val) @ w_down


def setup_inputs(seed: int = 0) -> dict:
    key = jax.random.key(seed)
    ks = jax.random.split(key, 28)
    f32 = jnp.float32
    L = DEPTH
    W = RWKV_WIDTH

    def nrm(k, shape, scale):
        return scale * jax.random.normal(k, shape, f32)

    x = nrm(ks[0], (BATCH, SEQ, D_MODEL), 1.0)
    c = nrm(ks[1], (BATCH, D_MODEL), 1.0)
    positions = (jnp.arange(SEQ, dtype=jnp.int32)[None, :]
                 + jax.random.randint(ks[2], (BATCH, 1), 0, 1024, dtype=jnp.int32))
    ada_w = nrm(ks[3], (L, D_MODEL, 6 * D_MODEL), 0.5 * D_MODEL ** -0.5)
    ada_b = nrm(ks[4], (L, 6 * D_MODEL), 0.02)
    w_in = nrm(ks[5], (L, D_MODEL, D_IN), D_MODEL ** -0.5)
    mla_q_norm = 1.0 + nrm(ks[6], (L, MLA_Q_RANK), 0.05)
    mla_w_uq = nrm(ks[7], (L, MLA_Q_RANK, MLA_HEADS * (MLA_NOPE + MLA_ROPE)), MLA_Q_RANK ** -0.5)
    mla_kv_norm = 1.0 + nrm(ks[8], (L, MLA_KV_RANK), 0.05)
    mla_w_ukv = nrm(ks[9], (L, MLA_KV_RANK, MLA_HEADS * (MLA_NOPE + MLA_V)), MLA_KV_RANK ** -0.5)
    mla_out_norm = 1.0 + nrm(ks[10], (L, MLA_WIDTH), 0.05)
    rwkv_mu = jax.random.uniform(ks[11], (L, 2, RWKV_IN), f32, 0.0, 0.5)
    rwkv_w0 = jax.random.uniform(ks[12], (L, 2, W), f32, -6.0, 1.0)
    rwkv_w2 = nrm(ks[13], (L, 2, DECAY_LORA, W), 0.5 * DECAY_LORA ** -0.5)
    rwkv_a0 = nrm(ks[14], (L, 2, W), 0.1)
    rwkv_a2 = nrm(ks[15], (L, 2, AAA_LORA, W), 0.5 * AAA_LORA ** -0.5)
    rwkv_g2 = nrm(ks[16], (L, GATE_LORA, W), GATE_LORA ** -0.5)
    rwkv_k_k = 0.85 + nrm(ks[17], (L, W), 0.05)
    rwkv_k_a = 1.0 + nrm(ks[18], (L, W), 0.05)
    rwkv_r_k = nrm(ks[19], (L, RWKV_HEADS, RWKV_HEAD), 0.1)
    rwkv_gn_g = 1.0 + nrm(ks[20], (L, W), 0.05)
    rwkv_gn_b = nrm(ks[21], (L, W), 0.02)
    w_out = nrm(ks[22], (L, D_MIX, D_MODEL), D_MIX ** -0.5)
    ffn_w_up = nrm(ks[23], (L, D_MODEL, 2 * D_FF), D_MODEL ** -0.5)
    ffn_conv_w = nrm(ks[24], (L, CONV_W, D_FF), CONV_W ** -0.5)
    ffn_conv_b = nrm(ks[25], (L, D_FF), 0.02)
    ffn_w_down = nrm(ks[26], (L, D_FF, D_MODEL), D_FF ** -0.5)
    final_norm = 1.0 + nrm(ks[27], (D_MODEL,), 0.05)
    return {'x': x, 'c': c, 'positions': positions, 'ada_w': ada_w, 'ada_b': ada_b, 'w_in': w_in,
            'mla_q_norm': mla_q_norm, 'mla_w_uq': mla_w_uq, 'mla_kv_norm': mla_kv_norm,
            'mla_w_ukv': mla_w_ukv, 'mla_out_norm': mla_out_norm, 'rwkv_mu': rwkv_mu,
            'rwkv_w0': rwkv_w0, 'rwkv_w2': rwkv_w2, 'rwkv_a0': rwkv_a0, 'rwkv_a2': rwkv_a2,
            'rwkv_g2': rwkv_g2, 'rwkv_k_k': rwkv_k_k, 'rwkv_k_a': rwkv_k_a, 'rwkv_r_k': rwkv_r_k,
            'rwkv_gn_g': rwkv_gn_g, 'rwkv_gn_b': rwkv_gn_b, 'w_out': w_out, 'ffn_w_up': ffn_w_up,
            'ffn_conv_w': ffn_conv_w, 'ffn_conv_b': ffn_conv_b, 'ffn_w_down': ffn_w_down,
            'final_norm': final_norm}


def reference(x, c, positions, ada_w, ada_b, w_in, mla_q_norm, mla_w_uq, mla_kv_norm, mla_w_ukv,
              mla_out_norm, rwkv_mu, rwkv_w0, rwkv_w2, rwkv_a0, rwkv_a2, rwkv_g2, rwkv_k_k,
              rwkv_k_a, rwkv_r_k, rwkv_gn_g, rwkv_gn_b, w_out, ffn_w_up, ffn_conv_w, ffn_conv_b,
              ffn_w_down, final_norm):
    c_act = jax.nn.silu(c)
    for l in range(DEPTH):
        mod = c_act @ ada_w[l] + ada_b[l]
        sh1, sc1, gt1, sh2, sc2, gt2 = [m[:, None, :] for m in jnp.split(mod, 6, axis=-1)]

        h = rms_norm(x) * (1 + sc1) + sh1
        z = h @ w_in[l]
        z_mla, z_rwkv = z[..., :MLA_IN], z[..., MLA_IN:]
        q_lat, kv_lat, k_pe = _split(z_mla, MLA_SIZES)
        z_rwkv = centred_token_shift(z_rwkv, rwkv_mu[l])
        r, k, v, xw_f, xw_b, xa_f, xa_b, xg = _split(z_rwkv, RWKV_SIZES)
        y_mla = mla_group(q_lat, kv_lat, k_pe, positions, mla_q_norm[l], mla_w_uq[l],
                          mla_kv_norm[l], mla_w_ukv[l], mla_out_norm[l])
        y_rwkv = rwkv7_group(r, k, v, xw_f, xw_b, xa_f, xa_b, xg, rwkv_w0[l], rwkv_w2[l],
                             rwkv_a0[l], rwkv_a2[l], rwkv_g2[l], rwkv_k_k[l], rwkv_k_a[l],
                             rwkv_r_k[l], rwkv_gn_g[l], rwkv_gn_b[l])
        x = x + gt1 * (jnp.concatenate([y_mla, y_rwkv], axis=-1) @ w_out[l])

        h = rms_norm(x) * (1 + sc2) + sh2
        x = x + gt2 * conv_ffn(h, ffn_w_up[l], ffn_conv_w[l], ffn_conv_b[l], ffn_w_down[l])
    return rms_norm(x, final_norm)
```

```python
import functools

import jax
import jax.numpy as jnp
from jax import lax
from jax.experimental import pallas as pl
from jax.experimental.pallas import tpu as pltpu

D_MODEL = 1024
DEPTH = 4
MLA_HEADS = 4
MLA_NOPE = 128
MLA_ROPE = 64
MLA_V = 128
MLA_Q_RANK = 256
MLA_KV_RANK = 128
MLA_WIDTH = MLA_HEADS * MLA_V
MLA_QK = MLA_NOPE + MLA_ROPE
ROPE_BASE = 10000.0
RWKV_WIDTH = D_MODEL - MLA_WIDTH
RWKV_HEAD = 64
RWKV_HEADS = RWKV_WIDTH // RWKV_HEAD
DECAY_LORA = 64
AAA_LORA = 64
GATE_LORA = 128
GN_EPS = 64e-5
D_FF = 2816
EPS = 1e-6

MLA_COLS = 512
RWKV_COLS = 3 * RWKV_WIDTH + GATE_LORA + 2 * DECAY_LORA + 2 * AAA_LORA
IN_COLS = MLA_COLS + RWKV_COLS
Q_HEAD_COLS = MLA_NOPE + 2 * MLA_ROPE

HALO = 8
VMEM_LIMIT = 56 * 1024 * 1024

F32 = jnp.float32
BF16 = jnp.bfloat16


def _sigmoid(x):
    return 1.0 / (1.0 + jnp.exp(-x))


def _rms(x, eps=EPS):
    return x * lax.rsqrt(jnp.mean(x * x, axis=-1, keepdims=True) + eps)


def _seg_dot(x, seg):
    hi = x.astype(BF16)
    lo = (x - hi.astype(F32)).astype(BF16)
    return (jnp.dot(hi, seg, preferred_element_type=F32)
            + jnp.dot(lo, seg, preferred_element_type=F32))


def _halo_bounds(i, nt, tm):
    lo = jnp.where(i > 0, 0, HALO)
    hi = jnp.where(i < nt - 1, tm + 2 * HALO, tm + HALO)
    return lo, hi


def _adaln_kernel(c_ref, w_ref, b_ref, o_ref):
    c = c_ref[...]
    ca = c * _sigmoid(c)
    o_ref[0] = jnp.dot(ca, w_ref[0], precision=lax.Precision.HIGHEST,
                       preferred_element_type=F32) + b_ref[0]


def _adaln(c, ada_w, ada_b):
    L, D, D6 = ada_w.shape
    B = c.shape[0]
    nj = D6 // D
    return pl.pallas_call(
        _adaln_kernel,
        grid=(L, nj),
        in_specs=[
            pl.BlockSpec((B, D), lambda l, j: (0, 0)),
            pl.BlockSpec((1, D, D), lambda l, j: (l, 0, j)),
            pl.BlockSpec((1, 1, D), lambda l, j: (l, 0, j)),
        ],
        out_specs=pl.BlockSpec((1, B, D), lambda l, j: (l, 0, j)),
        out_shape=jax.ShapeDtypeStruct((L, B, D6), F32),
        compiler_params=pltpu.CompilerParams(
            dimension_semantics=("arbitrary", "arbitrary"), vmem_limit_bytes=VMEM_LIMIT),
        name="adaln",
    )(c, ada_w, ada_b.reshape(L, 1, D6))


def _rope_kernel(pos_ref, freq_ref, o_ref):
    ang = pos_ref[0].astype(F32) * freq_ref[...]
    lane = lax.broadcasted_iota(jnp.int32, ang.shape, 1)
    cos = jnp.cos(ang)
    sin = jnp.sin(ang)
    o_ref[0] = jnp.where(lane < 64, cos, jnp.where(lane < 96, -sin, sin))


def _rope_table(positions, ts):
    B, S = positions.shape
    half = MLA_ROPE // 2
    inv_freq = ROPE_BASE ** (-jnp.arange(half, dtype=F32) / half)
    freq = jnp.tile(inv_freq, 4).reshape(1, 128)
    pos_b = jnp.broadcast_to(positions[:, :, None], (B, S, 128))
    return pl.pallas_call(
        _rope_kernel,
        grid=(B, S // ts),
        in_specs=[
            pl.BlockSpec((1, ts, 128), lambda b, i: (b, i, 0)),
            pl.BlockSpec((1, 128), lambda b, i: (0, 0)),
        ],
        out_specs=pl.BlockSpec((1, ts, 128), lambda b, i: (b, i, 0)),
        out_shape=jax.ShapeDtypeStruct((B, S, 128), F32),
        compiler_params=pltpu.CompilerParams(
            dimension_semantics=("arbitrary", "arbitrary"), vmem_limit_bytes=VMEM_LIMIT),
        name="rope_table",
    )(pos_b, freq)


def _inproj_kernel(xm_ref, xp_ref, xn_ref, mod_ref, rope_ref, w_in_ref, mu_ref,
                   qn_ref, wuq_ref, kvn_ref, wukv_ref,
                   w0_ref, w2_ref, a0_ref, a2_ref, g2_ref, kk_ref, ka_ref, rk_ref, seg_ref,
                   q_out, k_out, v_out, r_out, vv_out, nkk_out, wf_out, wb_out,
                   kdf_out, kdb_out, bf_out, bb_out, bonus_out, g_out, *, tm, nt):
    i = pl.program_id(1)
    D = D_MODEL
    W = RWKV_WIDTH
    te = tm + 2 * HALO
    x_ext = jnp.concatenate([xp_ref[0], xm_ref[0], xn_ref[0]], axis=0)
    mod = mod_ref[0]
    sh1 = mod[:, 0:D]
    sc1 = mod[:, D:2 * D]
    h = _rms(x_ext) * (1.0 + sc1) + sh1
    z = jnp.dot(h.astype(BF16), w_in_ref[...], preferred_element_type=F32)

    zr = z[:, MLA_COLS:]
    row = lax.broadcasted_iota(jnp.int32, (te, 1), 0)
    lo, hi = _halo_bounds(i, nt, tm)
    zr = jnp.where((row >= lo) & (row < hi), zr, 0.0)
    zc = zr[HALO:HALO + tm]
    zp = pltpu.roll(zr, 1, 0)[HALO:HALO + tm]
    zn = pltpu.roll(zr, te - 1, 0)[HALO:HALO + tm]
    mu = mu_ref[...]
    zs = zc + mu[0:1] * (zp - zc) + mu[1:2] * (zn - zc)

    zm = z[HALO:HALO + tm, 0:MLA_COLS]
    rope = rope_ref[0]
    hq = _rms(zm[:, 0:MLA_Q_RANK]) * qn_ref[...]
    q = jnp.dot(hq.astype(BF16), wuq_ref[...], preferred_element_type=F32)
    hkv = _rms(zm[:, MLA_Q_RANK:MLA_Q_RANK + MLA_KV_RANK]) * kvn_ref[...]
    kv = jnp.dot(hkv.astype(BF16), wukv_ref[...], preferred_element_type=F32)
    kprod = zm[:, 384:512] * rope
    kpe = (kprod + pltpu.roll(kprod, 64, 1))[:, 0:MLA_ROPE].astype(BF16)
    scale = MLA_QK ** -0.5
    for hd in range(MLA_HEADS):
        qh = q[:, hd * Q_HEAD_COLS:(hd + 1) * Q_HEAD_COLS]
        qprod = qh[:, MLA_NOPE:] * rope
        qpe = (qprod + pltpu.roll(qprod, 64, 1))[:, 0:MLA_ROPE]
        q_out[0, hd, :, 0:MLA_NOPE] = (qh[:, 0:MLA_NOPE] * scale).astype(BF16)
        q_out[0, hd, :, MLA_NOPE:MLA_QK] = (qpe * scale).astype(BF16)
        k_out[0, hd, :, 0:MLA_NOPE] = kv[:, hd * 256:hd * 256 + MLA_NOPE].astype(BF16)
        k_out[0, hd, :, MLA_NOPE:MLA_QK] = kpe
        v_out[0, hd] = kv[:, hd * 256 + MLA_NOPE:(hd + 1) * 256].astype(BF16)

    r = zs[:, 0:W]
    k = zs[:, W:2 * W]
    v = zs[:, 2 * W:3 * W]
    xg = zs[:, 3 * W:3 * W + GATE_LORA]
    xw = zs[:, 3 * W + GATE_LORA:3 * W + GATE_LORA + 2 * DECAY_LORA]
    xa = zs[:, 3 * W + GATE_LORA + 2 * DECAY_LORA:]
    seg = seg_ref[...]
    g_out[0] = jnp.dot(_sigmoid(xg).astype(BF16), g2_ref[...], preferred_element_type=F32)
    kk = k * kk_ref[...]
    kkn = kk * lax.rsqrt(_seg_dot(kk * kk, seg) + 1e-12)
    lw = jnp.dot(jnp.tanh(xw).astype(BF16), w2_ref[...], preferred_element_type=F32) + w0_ref[...]
    wl = jnp.minimum(lw, 0.0) - jnp.log(1.0 + jnp.exp(-jnp.abs(lw))) - 0.5
    w = jnp.exp(-jnp.exp(wl))
    a = _sigmoid(jnp.dot(xa.astype(BF16), a2_ref[...], preferred_element_type=F32) + a0_ref[...])
    ka = ka_ref[...]
    kd_f = k * (1.0 + (a[:, 0:W] - 1.0) * ka)
    kd_b = k * (1.0 + (a[:, W:] - 1.0) * ka)
    r_out[0] = r
    vv_out[0] = v
    nkk_out[0] = -kkn
    wf_out[0] = w[:, 0:W]
    wb_out[0] = w[:, W:]
    kdf_out[0] = kd_f
    kdb_out[0] = kd_b
    bf_out[0] = kkn * a[:, 0:W]
    bb_out[0] = kkn * a[:, W:]
    bonus_out[0] = _seg_dot(r * (0.5 * (kd_f + kd_b)) * rk_ref[...], seg) * v


def _inproj(x, mod_l, rope, lw, tm):
    B, S, D = x.shape
    nt = S // tm
    hb = tm // HALO
    nhb = S // HALO
    W = RWKV_WIDTH

    def full(a):
        return pl.BlockSpec(a.shape, lambda b, i: (0,) * a.ndim)

    weights = [lw["w_in"], lw["mu"], lw["q_norm"], lw["w_uq"], lw["kv_norm"], lw["w_ukv"],
               lw["w0"], lw["w2"], lw["a0"], lw["a2"], lw["g2"], lw["k_k"], lw["k_a"], lw["r_k"],
               lw["seg"]]
    tok = pl.BlockSpec((1, tm, W), lambda b, i: (b, i, 0))
    rw = jax.ShapeDtypeStruct((B, S, W), F32)
    out_shape = (
        jax.ShapeDtypeStruct((B, MLA_HEADS, S, MLA_QK), BF16),
        jax.ShapeDtypeStruct((B, MLA_HEADS, S, MLA_QK), BF16),
        jax.ShapeDtypeStruct((B, MLA_HEADS, S, MLA_V), BF16),
    ) + (rw,) * 11
    out_specs = (
        pl.BlockSpec((1, MLA_HEADS, tm, MLA_QK), lambda b, i: (b, 0, i, 0)),
        pl.BlockSpec((1, MLA_HEADS, tm, MLA_QK), lambda b, i: (b, 0, i, 0)),
        pl.BlockSpec((1, MLA_HEADS, tm, MLA_V), lambda b, i: (b, 0, i, 0)),
    ) + (tok,) * 11
    return pl.pallas_call(
        functools.partial(_inproj_kernel, tm=tm, nt=nt),
        grid=(B, nt),
        in_specs=[
            pl.BlockSpec((1, tm, D), lambda b, i: (b, i, 0)),
            pl.BlockSpec((1, HALO, D), lambda b, i: (b, jnp.maximum(i * hb - 1, 0), 0)),
            pl.BlockSpec((1, HALO, D), lambda b, i: (b, jnp.minimum((i + 1) * hb, nhb - 1), 0)),
            pl.BlockSpec((1, 1, 6 * D), lambda b, i: (b, 0, 0)),
            pl.BlockSpec((1, tm, 128), lambda b, i: (b, i, 0)),
        ] + [full(a) for a in weights],
        out_specs=out_specs,
        out_shape=out_shape,
        compiler_params=pltpu.CompilerParams(
            dimension_semantics=("arbitrary", "arbitrary"), vmem_limit_bytes=VMEM_LIMIT),
        name="inproj",
    )(x, x, x, mod_l, rope, *weights)


def _attn_kernel(q_ref, k_ref, v_ref, o_ref, *, tk, nk):
    q = q_ref[0, 0]
    tq = q.shape[0]

    def body(j, carry):
        m, l, acc = carry
        start = pl.multiple_of(j * tk, tk)
        kj = k_ref[0, 0, pl.ds(start, tk), :]
        vj = v_ref[0, 0, pl.ds(start, tk), :]
        s = lax.dot_general(q, kj, (((1,), (1,)), ((), ())), preferred_element_type=F32)
        m_new = jnp.maximum(m, jnp.max(s, axis=1, keepdims=True))
        p = jnp.exp(s - m_new)
        alpha = jnp.exp(m - m_new)
        l = alpha * l + jnp.sum(p, axis=1, keepdims=True)
        acc = acc * alpha + jnp.dot(p.astype(BF16), vj, preferred_element_type=F32)
        return m_new, l, acc

    m0 = jnp.full((tq, 1), -jnp.inf, F32)
    l0 = jnp.zeros((tq, 1), F32)
    acc0 = jnp.zeros((tq, MLA_V), F32)
    m, l, acc = lax.fori_loop(0, nk, body, (m0, l0, acc0))
    o_ref[0] = acc / l


def _attention(q, k, v, tq, tk):
    B, H, S, _ = q.shape
    return pl.pallas_call(
        functools.partial(_attn_kernel, tk=tk, nk=S // tk),
        grid=(B, H, S // tq),
        in_specs=[
            pl.BlockSpec((1, 1, tq, MLA_QK), lambda b, h, i: (b, h, i, 0)),
            pl.BlockSpec((1, 1, S, MLA_QK), lambda b, h, i: (b, h, 0, 0)),
            pl.BlockSpec((1, 1, S, MLA_V), lambda b, h, i: (b, h, 0, 0)),
        ],
        out_specs=pl.BlockSpec((1, tq, MLA_V), lambda b, h, i: (b, i, h)),
        out_shape=jax.ShapeDtypeStruct((B, S, H * MLA_V), F32),
        compiler_params=pltpu.CompilerParams(
            dimension_semantics=("arbitrary", "arbitrary", "arbitrary"),
            vmem_limit_bytes=VMEM_LIMIT),
        name="mla_attention",
    )(q, k, v)


def _scan_kernel(r_ref, w_ref, kd_ref, v_ref, b_ref, nk_ref, y_ref, state_ref, sa_ref, *, tb):
    N = RWKV_HEAD

    @pl.when(pl.program_id(0) == 0)
    def _():
        state_ref[...] = jnp.zeros_like(state_ref)
        sa_ref[...] = jnp.zeros_like(sa_ref)

    def step(j, _):
        sa = sa_ref[...]
        vv = v_ref[j]
        y = jnp.zeros_like(sa)
        sa_next = jnp.zeros_like(sa)
        for kc in range(N):
            row = pl.ds(kc, 1)
            s_new = (state_ref[kc] * w_ref[j, row, :] + sa * b_ref[j, row, :]
                     + vv * kd_ref[j, row, :])
            state_ref[kc] = s_new
            y = y + s_new * r_ref[j, row, :]
            sa_next = sa_next + s_new * nk_ref[j, row, :]
        y_ref[j] = y
        sa_ref[...] = sa_next
        return 0

    lax.fori_loop(0, tb, step, 0)


def _scan(r_t, w_t, kd_t, v_t, b_t, nk_t, tb):
    S, N, C = r_t.shape
    blk = pl.BlockSpec((tb, N, C), lambda i: (i, 0, 0))
    return pl.pallas_call(
        functools.partial(_scan_kernel, tb=tb),
        grid=(S // tb,),
        in_specs=[blk] * 6,
        out_specs=blk,
        out_shape=jax.ShapeDtypeStruct((S, N, C), F32),
        scratch_shapes=[pltpu.VMEM((N, N, C), F32), pltpu.VMEM((N, C), F32)],
        compiler_params=pltpu.CompilerParams(
            dimension_semantics=("arbitrary",), vmem_limit_bytes=VMEM_LIMIT),
        name="rwkv7_scan",
    )(r_t, w_t, kd_t, v_t, b_t, nk_t)


def _mixout_kernel(x_ref, o_ref, yf_ref, yb_ref, bonus_ref, g_ref, mod_ref, on_ref, gng_ref,
                   gnb_ref, seg_ref, wo_ref, out_ref):
    D = D_MODEL
    gt1 = mod_ref[0][:, 2 * D:3 * D]
    seg = seg_ref[...]
    y_mla = _rms(o_ref[0]) * on_ref[...]
    y = yf_ref[0] + yb_ref[0]
    inv_n = 1.0 / RWKV_HEAD
    d = y - _seg_dot(y, seg) * inv_n
    var = _seg_dot(d * d, seg) * inv_n
    yn = d * lax.rsqrt(var + GN_EPS) * gng_ref[...] + gnb_ref[...]
    y_rwkv = (yn + bonus_ref[0]) * g_ref[0]
    proj = (jnp.dot(y_mla.astype(BF16), wo_ref[0:MLA_WIDTH, :], preferred_element_type=F32)
            + jnp.dot(y_rwkv.astype(BF16), wo_ref[MLA_WIDTH:, :], preferred_element_type=F32))
    out_ref[0] = x_ref[0] + gt1 * proj


def _mixout(x, o, y_f, y_b, bonus, g, mod_l, lw, tm):
    B, S, D = x.shape
    W = RWKV_WIDTH

    def full(a):
        return pl.BlockSpec(a.shape, lambda b, i: (0,) * a.ndim)

    weights = [lw["out_norm"], lw["gn_g"], lw["gn_b"], lw["seg"], lw["w_out"]]
    tokw = pl.BlockSpec((1, tm, W), lambda b, i: (b, i, 0))
    tokd = pl.BlockSpec((1, tm, D), lambda b, i: (b, i, 0))
    return pl.pallas_call(
        _mixout_kernel,
        grid=(B, S // tm),
        in_specs=[tokd, tokw, tokw, tokw, tokw, tokw,
                  pl.BlockSpec((1, 1, 6 * D), lambda b, i: (b, 0, 0))] + [full(a) for a in weights],
        out_specs=tokd,
        out_shape=jax.ShapeDtypeStruct((B, S, D), F32),
        compiler_params=pltpu.CompilerParams(
            dimension_semantics=("arbitrary", "arbitrary"), vmem_limit_bytes=VMEM_LIMIT),
        name="mixer_out",
    )(x, o, y_f, y_b, bonus, g, mod_l, *weights)


def _ffn_kernel(xm_ref, xp_ref, xn_ref, mod_ref, wg_ref, wv_ref, cw_ref, cb_ref, wd_ref,
                out_ref, h_ref, acc_ref, *, tm, nt, nc):
    i = pl.program_id(1)
    c = pl.program_id(2)
    D = D_MODEL
    te = tm + 2 * HALO
    mod = mod_ref[0]

    @pl.when(c == 0)
    def _():
        x_ext = jnp.concatenate([xp_ref[0], xm_ref[0], xn_ref[0]], axis=0)
        sh2 = mod[:, 3 * D:4 * D]
        sc2 = mod[:, 4 * D:5 * D]
        h_ref[...] = (_rms(x_ext) * (1.0 + sc2) + sh2).astype(BF16)
        acc_ref[...] = jnp.zeros_like(acc_ref)

    h = h_ref[...]
    gate = jnp.dot(h, wg_ref[...], preferred_element_type=F32)
    val = jnp.dot(h[HALO:HALO + tm], wv_ref[...], preferred_element_type=F32)
    row = lax.broadcasted_iota(jnp.int32, (te, 1), 0)
    lo, hi = _halo_bounds(i, nt, tm)
    gate = jnp.where((row >= lo) & (row < hi), gate, 0.0)
    gp = pltpu.roll(gate, 1, 0)[HALO:HALO + tm]
    gn = pltpu.roll(gate, te - 1, 0)[HALO:HALO + tm]
    gc = gate[HALO:HALO + tm]
    cw = cw_ref[...]
    gg = cw[0:1] * gp + cw[1:2] * gc + cw[2:3] * gn + cb_ref[...]
    act = gg * _sigmoid(gg) * val
    acc_ref[...] += jnp.dot(act.astype(BF16), wd_ref[...], preferred_element_type=F32)

    @pl.when(c == nc - 1)
    def _():
        gt2 = mod[:, 5 * D:6 * D]
        out_ref[0] = xm_ref[0] + gt2 * acc_ref[...]


def _ffn(x, mod_l, lw, tm, fc):
    B, S, D = x.shape
    nt = S // tm
    nc = D_FF // fc
    hb = tm // HALO
    nhb = S // HALO
    return pl.pallas_call(
        functools.partial(_ffn_kernel, tm=tm, nt=nt, nc=nc),
        grid=(B, nt, nc),
        in_specs=[
            pl.BlockSpec((1, tm, D), lambda b, i, c: (b, i, 0)),
            pl.BlockSpec((1, HALO, D), lambda b, i, c: (b, jnp.maximum(i * hb - 1, 0), 0)),
            pl.BlockSpec((1, HALO, D), lambda b, i, c: (b, jnp.minimum((i + 1) * hb, nhb - 1), 0)),
            pl.BlockSpec((1, 1, 6 * D), lambda b, i, c: (b, 0, 0)),
            pl.BlockSpec((D, fc), lambda b, i, c: (0, c)),
            pl.BlockSpec((D, fc), lambda b, i, c: (0, nc + c)),
            pl.BlockSpec((3, fc), lambda b, i, c: (0, c)),
            pl.BlockSpec((1, fc), lambda b, i, c: (0, c)),
            pl.BlockSpec((fc, D), lambda b, i, c: (c, 0)),
        ],
        out_specs=pl.BlockSpec((1, tm, D), lambda b, i, c: (b, i, 0)),
        out_shape=jax.ShapeDtypeStruct((B, S, D), F32),
        scratch_shapes=[pltpu.VMEM((tm + 2 * HALO, D), BF16), pltpu.VMEM((tm, D), F32)],
        compiler_params=pltpu.CompilerParams(
            dimension_semantics=("arbitrary", "arbitrary", "arbitrary"),
            vmem_limit_bytes=VMEM_LIMIT),
        name="conv_ffn",
    )(x, x, x, mod_l, lw["w_up"], lw["w_up"], lw["conv_w"], lw["conv_b"], lw["w_down"])


def _final_norm_kernel(x_ref, g_ref, o_ref):
    o_ref[0] = _rms(x_ref[0]) * g_ref[...]


def _final_norm(x, gain, tm):
    B, S, D = x.shape
    tok = pl.BlockSpec((1, tm, D), lambda b, i: (b, i, 0))
    return pl.pallas_call(
        _final_norm_kernel,
        grid=(B, S // tm),
        in_specs=[tok, pl.BlockSpec((1, D), lambda b, i: (0, 0))],
        out_specs=tok,
        out_shape=jax.ShapeDtypeStruct((B, S, D), F32),
        compiler_params=pltpu.CompilerParams(
            dimension_semantics=("arbitrary", "arbitrary"), vmem_limit_bytes=VMEM_LIMIT),
        name="final_norm",
    )(x, gain.reshape(1, D))


def _block_diag2(a, b):
    za = jnp.zeros((a.shape[0], b.shape[1]), a.dtype)
    zb = jnp.zeros((b.shape[0], a.shape[1]), a.dtype)
    return jnp.concatenate([jnp.concatenate([a, za], axis=1),
                            jnp.concatenate([zb, b], axis=1)], axis=0)


def _swap_halves(cols):
    half = cols.shape[-1] // 2
    return jnp.concatenate([cols[..., half:], cols[..., :half]], axis=-1)


def _layer_weights(l, p):
    W = RWKV_WIDTH
    w_in = p["w_in"][l]
    mla = w_in[:, :448]
    rw = w_in[:, 448:]
    kpe = mla[:, 384:448]
    rw_perm = jnp.concatenate([rw[:, :3 * W], rw[:, 3 * W + 256:], rw[:, 3 * W:3 * W + 256]], axis=1)
    mu = p["rwkv_mu"][l]
    mu_perm = jnp.concatenate([mu[:, :3 * W], mu[:, 3 * W + 256:], mu[:, 3 * W:3 * W + 256]], axis=1)
    w_in_perm = jnp.concatenate([mla, _swap_halves(kpe), rw_perm], axis=1).astype(BF16)

    w_uq = p["mla_w_uq"][l].reshape(MLA_Q_RANK, MLA_HEADS, MLA_QK)
    pe = w_uq[:, :, MLA_NOPE:]
    w_uq_perm = jnp.concatenate([w_uq, _swap_halves(pe)], axis=-1).reshape(
        MLA_Q_RANK, MLA_HEADS * Q_HEAD_COLS).astype(BF16)

    head = jnp.arange(W) // RWKV_HEAD
    seg = (head[:, None] == head[None, :]).astype(BF16)
    return {
        "w_in": w_in_perm,
        "mu": mu_perm,
        "q_norm": p["mla_q_norm"][l].reshape(1, -1),
        "w_uq": w_uq_perm,
        "kv_norm": p["mla_kv_norm"][l].reshape(1, -1),
        "w_ukv": p["mla_w_ukv"][l].astype(BF16),
        "out_norm": p["mla_out_norm"][l].reshape(1, -1),
        "w0": p["rwkv_w0"][l].reshape(1, 2 * W),
        "w2": _block_diag2(p["rwkv_w2"][l, 0], p["rwkv_w2"][l, 1]).astype(BF16),
        "a0": p["rwkv_a0"][l].reshape(1, 2 * W),
        "a2": _block_diag2(p["rwkv_a2"][l, 0], p["rwkv_a2"][l, 1]).astype(BF16),
        "g2": p["rwkv_g2"][l].astype(BF16),
        "k_k": p["rwkv_k_k"][l].reshape(1, W),
        "k_a": p["rwkv_k_a"][l].reshape(1, W),
        "r_k": p["rwkv_r_k"][l].reshape(1, W),
        "gn_g": p["rwkv_gn_g"][l].reshape(1, W),
        "gn_b": p["rwkv_gn_b"][l].reshape(1, W),
        "seg": seg,
        "w_out": p["w_out"][l].astype(BF16),
        "w_up": p["ffn_w_up"][l].astype(BF16),
        "conv_w": p["ffn_conv_w"][l],
        "conv_b": p["ffn_conv_b"][l].reshape(1, D_FF),
        "w_down": p["ffn_w_down"][l].astype(BF16),
    }


def _to_chains(fwd, bwd):
    B, S, _ = fwd.shape

    def tr(a):
        return a.reshape(B, S, RWKV_HEADS, RWKV_HEAD).transpose(1, 3, 0, 2).reshape(
            S, RWKV_HEAD, B * RWKV_HEADS)

    return jnp.concatenate([tr(fwd), tr(bwd)[::-1]], axis=-1)


def _from_chains(y, B):
    S = y.shape[0]
    C = B * RWKV_HEADS

    def tr(a):
        return a.reshape(S, RWKV_HEAD, B, RWKV_HEADS).transpose(2, 0, 3, 1).reshape(B, S, RWKV_WIDTH)

    return tr(y[:, :, :C]), tr(y[::-1, :, C:])


def _forward(x, c, positions, p, cfg):
    B, S, D = x.shape
    mod = _adaln(c, p["ada_w"], p["ada_b"])
    rope = _rope_table(positions, cfg["tm_in"])
    for l in range(DEPTH):
        lw = _layer_weights(l, p)
        mod_l = mod[l].reshape(B, 1, 6 * D)
        (q, k, v, r, vv, nkk, w_f, w_b, kd_f, kd_b, b_f, b_b, bonus, g) = _inproj(
            x, mod_l, rope, lw, cfg["tm_in"])
        o = _attention(q, k, v, cfg["tq"], cfg["tk"])
        nk_t = _to_chains(nkk, nkk)
        nk_next = jnp.concatenate([nk_t[1:], jnp.zeros_like(nk_t[:1])], axis=0)
        y_t = _scan(_to_chains(r, r), _to_chains(w_f, w_b), _to_chains(kd_f, kd_b),
                    _to_chains(vv, vv), _to_chains(b_f, b_b), nk_next, cfg["tb"])
        y_f, y_b = _from_chains(y_t, B)
        x = _mixout(x, o, y_f, y_b, bonus, g, mod_l, lw, cfg["tm_out"])
        x = _ffn(x, mod_l, lw, cfg["tm_ffn"], cfg["fc"])
    return _final_norm(x, p["final_norm"], cfg["tm_out"])


_CFG = {"tm_in": 256, "tq": 512, "tk": 512, "tb": 32, "tm_out": 512, "tm_ffn": 512, "fc": 1408}


def kernel(x, c, positions, ada_w, ada_b, w_in, mla_q_norm, mla_w_uq, mla_kv_norm, mla_w_ukv, mla_out_norm, rwkv_mu, rwkv_w0, rwkv_w2, rwkv_a0, rwkv_a2, rwkv_g2, rwkv_k_k, rwkv_k_a, rwkv_r_k, rwkv_gn_g, rwkv_gn_b, w_out, ffn_w_up, ffn_conv_w, ffn_conv_b, ffn_w_down, final_norm):
    p = dict(ada_w=ada_w, ada_b=ada_b, w_in=w_in, mla_q_norm=mla_q_norm, mla_w_uq=mla_w_uq,
             mla_kv_norm=mla_kv_norm, mla_w_ukv=mla_w_ukv, mla_out_norm=mla_out_norm,
             rwkv_mu=rwkv_mu, rwkv_w0=rwkv_w0, rwkv_w2=rwkv_w2, rwkv_a0=rwkv_a0, rwkv_a2=rwkv_a2,
             rwkv_g2=rwkv_g2, rwkv_k_k=rwkv_k_k, rwkv_k_a=rwkv_k_a, rwkv_r_k=rwkv_r_k,
             rwkv_gn_g=rwkv_gn_g, rwkv_gn_b=rwkv_gn_b, w_out=w_out, ffn_w_up=ffn_w_up,
             ffn_conv_w=ffn_conv_w, ffn_conv_b=ffn_conv_b, ffn_w_down=ffn_w_down,
             final_norm=final_norm)
    return _forward(x, c, positions, p, _CFG)
```

```python
import functools

import jax
import jax.numpy as jnp
from jax import lax
from jax.experimental import pallas as pl
from jax.experimental.pallas import tpu as pltpu

D_MODEL = 1024
DEPTH = 4
MLA_HEADS = 4
MLA_NOPE = 128
MLA_ROPE = 64
MLA_V = 128
MLA_Q_RANK = 256
MLA_KV_RANK = 128
MLA_WIDTH = MLA_HEADS * MLA_V
MLA_QK = MLA_NOPE + MLA_ROPE
ROPE_BASE = 10000.0
RWKV_WIDTH = D_MODEL - MLA_WIDTH
RWKV_HEAD = 64
RWKV_HEADS = RWKV_WIDTH // RWKV_HEAD
DECAY_LORA = 64
AAA_LORA = 64
GATE_LORA = 128
GN_EPS = 64e-5
D_FF = 2816
EPS = 1e-6

MLA_COLS = 512
RWKV_COLS = 3 * RWKV_WIDTH + GATE_LORA + 2 * DECAY_LORA + 2 * AAA_LORA
IN_COLS = MLA_COLS + RWKV_COLS
Q_HEAD_ROWS = MLA_NOPE + 2 * MLA_ROPE
BF16_ROWS = 16
V_ROWS = MLA_V + BF16_ROWS

LANES = 128
HALO = 8
VMEM_LIMIT = 56 * 1024 * 1024
LOG2E = 1.4426950408889634

F32 = jnp.float32
BF16 = jnp.bfloat16
NT_DIMS = (((1,), (1,)), ((), ()))


def _sigmoid(x):
    return 1.0 / (1.0 + jnp.exp(-x))


def _rms(x, eps=EPS):
    return x * lax.rsqrt(jnp.mean(x * x, axis=-1, keepdims=True) + eps)


def _seg_dot(x, seg):
    hi = x.astype(BF16)
    lo = (x - hi.astype(F32)).astype(BF16)
    return (jnp.dot(hi, seg, preferred_element_type=F32)
            + jnp.dot(lo, seg, preferred_element_type=F32))


def _halo_bounds(i, nt, tm):
    lo = jnp.where(i > 0, 0, HALO)
    hi = jnp.where(i < nt - 1, tm + 2 * HALO, tm + HALO)
    return lo, hi


def _adaln_kernel(c_ref, w_ref, b_ref, o_ref):
    c = c_ref[...]
    ca = c * _sigmoid(c)
    o_ref[0] = jnp.dot(ca, w_ref[0], precision=lax.Precision.HIGHEST,
                       preferred_element_type=F32) + b_ref[0]


def _adaln(c, ada_w, ada_b):
    L, D, D6 = ada_w.shape
    B = c.shape[0]
    nj = D6 // D
    return pl.pallas_call(
        _adaln_kernel,
        grid=(L, nj),
        in_specs=[
            pl.BlockSpec((B, D), lambda l, j: (0, 0)),
            pl.BlockSpec((1, D, D), lambda l, j: (l, 0, j)),
            pl.BlockSpec((1, 1, D), lambda l, j: (l, 0, j)),
        ],
        out_specs=pl.BlockSpec((1, B, D), lambda l, j: (l, 0, j)),
        out_shape=jax.ShapeDtypeStruct((L, B, D6), F32),
        compiler_params=pltpu.CompilerParams(
            dimension_semantics=("arbitrary", "arbitrary"), vmem_limit_bytes=VMEM_LIMIT),
        name="adaln",
    )(c, ada_w, ada_b.reshape(L, 1, D6))


def _rope_kernel(pos_ref, freq_ref, tab_t_ref, tab_ref):
    ang = freq_ref[...] * pos_ref[0].astype(F32)
    row = lax.broadcasted_iota(jnp.int32, ang.shape, 0)
    cos = jnp.cos(ang)
    sin = jnp.sin(ang)
    tab_t = jnp.where(row < 64, cos, jnp.where(row < 96, -sin, sin))
    tab_t_ref[0] = tab_t
    tab_ref[0] = tab_t.T


def _rope_tables(positions, ts):
    B, S = positions.shape
    half = MLA_ROPE // 2
    inv_freq = ROPE_BASE ** (-jnp.arange(half, dtype=F32) / half)
    freq = jnp.tile(inv_freq, 4).reshape(LANES, 1)
    return pl.pallas_call(
        _rope_kernel,
        grid=(B, S // ts),
        in_specs=[
            pl.BlockSpec((1, 1, ts), lambda b, i: (b, 0, i)),
            pl.BlockSpec((LANES, 1), lambda b, i: (0, 0)),
        ],
        out_specs=(pl.BlockSpec((1, LANES, ts), lambda b, i: (b, 0, i)),
                   pl.BlockSpec((1, ts, LANES), lambda b, i: (b, i, 0))),
        out_shape=(jax.ShapeDtypeStruct((B, LANES, S), F32),
                   jax.ShapeDtypeStruct((B, S, LANES), F32)),
        compiler_params=pltpu.CompilerParams(
            dimension_semantics=("arbitrary", "arbitrary"), vmem_limit_bytes=VMEM_LIMIT),
        name="rope_tables",
    )(positions.reshape(B, 1, S), freq)


def _store_heads(out_ref, val):
    for hd in range(RWKV_HEADS):
        out_ref[0, hd] = val[:, hd * RWKV_HEAD:(hd + 1) * RWKV_HEAD]


def _inproj_kernel(xm_ref, xp_ref, xn_ref, mod_ref, rope_ref, rope_t_ref, w_in_ref, mu_ref,
                   qn_ref, wuq_t_ref, kvn_ref, wuk_ref, wuv_t_ref,
                   w0_ref, w2_ref, a0_ref, a2_ref, g2_ref, kk_ref, ka_ref, rk_ref, seg_ref,
                   qt_out, k_out, vt_out, r_out, vv_out, nkk_out, wf_out, wb_out,
                   kdf_out, kdb_out, bf_out, bb_out, bonus_out, g_out, *, tm, nt):
    i = pl.program_id(1)
    D = D_MODEL
    W = RWKV_WIDTH
    te = tm + 2 * HALO
    x_ext = jnp.concatenate([xp_ref[0], xm_ref[0], xn_ref[0]], axis=0)
    mod = mod_ref[0]
    sh1 = mod[:, 0:D]
    sc1 = mod[:, D:2 * D]
    h = _rms(x_ext) * (1.0 + sc1) + sh1
    z = jnp.dot(h.astype(BF16), w_in_ref[...], preferred_element_type=F32)

    zr = z[:, MLA_COLS:]
    row = lax.broadcasted_iota(jnp.int32, (te, 1), 0)
    lo, hi = _halo_bounds(i, nt, tm)
    zr = jnp.where((row >= lo) & (row < hi), zr, 0.0)
    zc = zr[HALO:HALO + tm]
    zp = pltpu.roll(zr, 1, 0)[HALO:HALO + tm]
    zn = pltpu.roll(zr, te - 1, 0)[HALO:HALO + tm]
    mu = mu_ref[...]
    zs = zc + mu[0:1] * (zp - zc) + mu[1:2] * (zn - zc)

    zm = z[HALO:HALO + tm, 0:MLA_COLS]
    hq = (_rms(zm[:, 0:MLA_Q_RANK]) * qn_ref[...]).astype(BF16)
    q_t = lax.dot_general(wuq_t_ref[...], hq, NT_DIMS, preferred_element_type=F32)
    hkv = (_rms(zm[:, MLA_Q_RANK:MLA_Q_RANK + MLA_KV_RANK]) * kvn_ref[...]).astype(BF16)
    kn = jnp.dot(hkv, wuk_ref[...], preferred_element_type=F32)
    v_t = lax.dot_general(wuv_t_ref[...], hkv, NT_DIMS, preferred_element_type=F32)
    kprod = zm[:, 384:512] * rope_ref[0]
    kpe = (kprod + pltpu.roll(kprod, 64, 1))[:, 0:MLA_ROPE].astype(BF16)
    rope_t = rope_t_ref[0]
    qscale = (MLA_QK ** -0.5) * LOG2E
    for hd in range(MLA_HEADS):
        qh = q_t[hd * Q_HEAD_ROWS:(hd + 1) * Q_HEAD_ROWS]
        qprod = qh[MLA_NOPE:] * rope_t
        qpe = qprod[0:MLA_ROPE] + qprod[MLA_ROPE:]
        qt_out[0, hd, 0:MLA_NOPE, :] = (qh[0:MLA_NOPE] * qscale).astype(BF16)
        qt_out[0, hd, MLA_NOPE:MLA_QK, :] = (qpe * qscale).astype(BF16)
        k_out[0, hd, :, 0:MLA_NOPE] = kn[:, hd * MLA_NOPE:(hd + 1) * MLA_NOPE].astype(BF16)
        k_out[0, hd, :, MLA_NOPE:MLA_QK] = kpe
        vt_out[0, hd, 0:MLA_V, :] = v_t[hd * MLA_V:(hd + 1) * MLA_V].astype(BF16)
        vt_out[0, hd, MLA_V:V_ROWS, :] = jnp.ones((BF16_ROWS, tm), BF16)

    r = zs[:, 0:W]
    k = zs[:, W:2 * W]
    v = zs[:, 2 * W:3 * W]
    xg = zs[:, 3 * W:3 * W + GATE_LORA]
    xw = zs[:, 3 * W + GATE_LORA:3 * W + GATE_LORA + 2 * DECAY_LORA]
    xa = zs[:, 3 * W + GATE_LORA + 2 * DECAY_LORA:]
    seg = seg_ref[...]
    g_out[0] = jnp.dot(_sigmoid(xg).astype(BF16), g2_ref[...], preferred_element_type=F32)
    kk = k * kk_ref[...]
    kkn = kk * lax.rsqrt(_seg_dot(kk * kk, seg) + 1e-12)
    lw = jnp.dot(jnp.tanh(xw).astype(BF16), w2_ref[...], preferred_element_type=F32) + w0_ref[...]
    wl = jnp.minimum(lw, 0.0) - jnp.log(1.0 + jnp.exp(-jnp.abs(lw))) - 0.5
    w = jnp.exp(-jnp.exp(wl))
    a = _sigmoid(jnp.dot(xa.astype(BF16), a2_ref[...], preferred_element_type=F32) + a0_ref[...])
    ka = ka_ref[...]
    kd_f = k * (1.0 + (a[:, 0:W] - 1.0) * ka)
    kd_b = k * (1.0 + (a[:, W:] - 1.0) * ka)
    _store_heads(r_out, r)
    _store_heads(vv_out, v)
    _store_heads(nkk_out, -kkn)
    _store_heads(wf_out, w[:, 0:W])
    _store_heads(wb_out, w[:, W:])
    _store_heads(kdf_out, kd_f)
    _store_heads(kdb_out, kd_b)
    _store_heads(bf_out, kkn * a[:, 0:W])
    _store_heads(bb_out, kkn * a[:, W:])
    bonus_out[0] = _seg_dot(r * (0.5 * (kd_f + kd_b)) * rk_ref[...], seg) * v


def _inproj(x, mod_l, rope, rope_t, lw, tm):
    B, S, D = x.shape
    nt = S // tm
    hb = tm // HALO
    nhb = S // HALO
    W = RWKV_WIDTH
    H = MLA_HEADS

    def full(a):
        return pl.BlockSpec(a.shape, lambda b, i: (0,) * a.ndim)

    weights = [lw["w_in"], lw["mu"], lw["q_norm"], lw["w_uq_t"], lw["kv_norm"], lw["w_uk"],
               lw["w_uv_t"], lw["w0"], lw["w2"], lw["a0"], lw["a2"], lw["g2"], lw["k_k"],
               lw["k_a"], lw["r_k"], lw["seg"]]
    tok = pl.BlockSpec((1, tm, W), lambda b, i: (b, i, 0))
    heads = pl.BlockSpec((1, RWKV_HEADS, tm, RWKV_HEAD), lambda b, i: (b, 0, i, 0))
    tok_shape = jax.ShapeDtypeStruct((B, S, W), F32)
    heads_shape = jax.ShapeDtypeStruct((B, RWKV_HEADS, S, RWKV_HEAD), F32)
    out_shape = (
        jax.ShapeDtypeStruct((B, H, MLA_QK, S), BF16),
        jax.ShapeDtypeStruct((B, H, S, MLA_QK), BF16),
        jax.ShapeDtypeStruct((B, H, V_ROWS, S), BF16),
    ) + (heads_shape,) * 9 + (tok_shape,) * 2
    out_specs = (
        pl.BlockSpec((1, H, MLA_QK, tm), lambda b, i: (b, 0, 0, i)),
        pl.BlockSpec((1, H, tm, MLA_QK), lambda b, i: (b, 0, i, 0)),
        pl.BlockSpec((1, H, V_ROWS, tm), lambda b, i: (b, 0, 0, i)),
    ) + (heads,) * 9 + (tok,) * 2
    return pl.pallas_call(
        functools.partial(_inproj_kernel, tm=tm, nt=nt),
        grid=(B, nt),
        in_specs=[
            pl.BlockSpec((1, tm, D), lambda b, i: (b, i, 0)),
            pl.BlockSpec((1, HALO, D), lambda b, i: (b, jnp.maximum(i * hb - 1, 0), 0)),
            pl.BlockSpec((1, HALO, D), lambda b, i: (b, jnp.minimum((i + 1) * hb, nhb - 1), 0)),
            pl.BlockSpec((1, 1, 6 * D), lambda b, i: (b, 0, 0)),
            pl.BlockSpec((1, tm, LANES), lambda b, i: (b, i, 0)),
            pl.BlockSpec((1, LANES, tm), lambda b, i: (b, 0, i)),
        ] + [full(a) for a in weights],
        out_specs=out_specs,
        out_shape=out_shape,
        compiler_params=pltpu.CompilerParams(
            dimension_semantics=("arbitrary", "arbitrary"), vmem_limit_bytes=VMEM_LIMIT),
        name="inproj",
    )(x, x, x, mod_l, rope, rope_t, *weights)


def _attn_kernel(qt_ref, k_ref, vt_ref, o_ref, s_ref, acc_ref, m_ref, *, tk, nk):
    q_t = qt_ref[0, 0]

    def scores(j, slot):
        start = pl.multiple_of(j * tk, tk)
        s_ref[slot] = jnp.dot(k_ref[0, 0, pl.ds(start, tk), :], q_t,
                              preferred_element_type=F32)

    def consume(j, slot):
        start = pl.multiple_of(j * tk, tk)
        s_t = s_ref[slot]
        m = m_ref[...]
        m_new = jnp.maximum(m, jnp.max(s_t, axis=0, keepdims=True))
        p_t = jnp.exp2(s_t - m_new).astype(BF16)
        alpha = jnp.exp2(m - m_new)
        acc_ref[...] = acc_ref[...] * alpha + jnp.dot(
            vt_ref[0, 0, :, pl.ds(start, tk)], p_t, preferred_element_type=F32)
        m_ref[...] = m_new

    m_ref[...] = jnp.full(m_ref.shape, -jnp.inf, F32)
    acc_ref[...] = jnp.zeros_like(acc_ref)
    scores(0, 0)

    def body(i, _):
        j = 2 * i
        scores(j + 1, 1)
        consume(j, 0)
        scores(jnp.minimum(j + 2, nk - 1), 0)
        consume(j + 1, 1)
        return 0

    lax.fori_loop(0, nk // 2, body, 0)
    acc = acc_ref[...]
    o_ref[0] = (acc[0:MLA_V] / acc[MLA_V:MLA_V + 1]).T


def _attention(q_t, k, v_t, tq, tk):
    B, H, S, _ = k.shape
    nk = S // tk
    assert nk % 2 == 0
    return pl.pallas_call(
        functools.partial(_attn_kernel, tk=tk, nk=nk),
        grid=(B, H, S // tq),
        in_specs=[
            pl.BlockSpec((1, 1, MLA_QK, tq), lambda b, h, i: (b, h, 0, i)),
            pl.BlockSpec((1, 1, S, MLA_QK), lambda b, h, i: (b, h, 0, 0)),
            pl.BlockSpec((1, 1, V_ROWS, S), lambda b, h, i: (b, h, 0, 0)),
        ],
        out_specs=pl.BlockSpec((1, tq, MLA_V), lambda b, h, i: (b, i, h)),
        out_shape=jax.ShapeDtypeStruct((B, S, H * MLA_V), F32),
        scratch_shapes=[pltpu.VMEM((2, tk, tq), F32), pltpu.VMEM((V_ROWS, tq), F32),
                        pltpu.VMEM((1, tq), F32)],
        compiler_params=pltpu.CompilerParams(
            dimension_semantics=("arbitrary", "arbitrary", "arbitrary"),
            vmem_limit_bytes=VMEM_LIMIT),
        name="mla_attention",
    )(q_t, k, v_t)


def _scan_kernel(rf_ref, rb_ref, wf_ref, wb_ref, kdf_ref, kdb_ref, vf_ref, vb_ref, bf_ref, bb_ref,
                 nkf_ref, nkb_ref, yf_ref, yb_ref,
                 r_s, w_s, kd_s, v_s, b_s, nk_s, y_s, state_ref, sa_ref, *, tb):
    N = RWKV_HEAD
    C = r_s.shape[2]
    Ch = C // 2
    per = LANES // N

    @pl.when(pl.program_id(0) == 0)
    def _():
        state_ref[...] = jnp.zeros_like(state_ref)

    def load(f_ref, b_ref, dst):
        for c in range(tb // per):
            cols = slice(c * LANES, (c + 1) * LANES)
            slab = jnp.concatenate([f_ref[:, cols], b_ref[:, cols]], axis=0).T
            for u in range(per):
                t = c * per + u
                blk = slab[u * N:(u + 1) * N]
                dst[t, :, 0:Ch] = blk[:, 0:Ch]
                dst[tb - 1 - t, :, Ch:C] = blk[:, Ch:C]

    load(rf_ref, rb_ref, r_s)
    load(wf_ref, wb_ref, w_s)
    load(kdf_ref, kdb_ref, kd_s)
    load(vf_ref, vb_ref, v_s)
    load(bf_ref, bb_ref, b_s)
    load(nkf_ref, nkb_ref, nk_s)

    sa0 = jnp.zeros((N, C), F32)
    for kc in range(N):
        sa0 = sa0 + state_ref[kc] * nk_s[0, pl.ds(kc, 1), :]
    sa_ref[...] = sa0

    def step(j, _):
        jn = jnp.minimum(j + 1, tb - 1)
        sa = sa_ref[...]
        vv = v_s[j]
        y = jnp.zeros_like(sa)
        sa_next = jnp.zeros_like(sa)
        for kc in range(N):
            row = pl.ds(kc, 1)
            s_new = (state_ref[kc] * w_s[j, row, :] + sa * b_s[j, row, :]
                     + vv * kd_s[j, row, :])
            state_ref[kc] = s_new
            y = y + s_new * r_s[j, row, :]
            sa_next = sa_next + s_new * nk_s[jn, row, :]
        y_s[j, :, 0:Ch] = y[:, 0:Ch]
        y_s[tb - 1 - j, :, Ch:C] = y[:, Ch:C]
        sa_ref[...] = sa_next
        return 0

    lax.fori_loop(0, tb, step, 0)

    for c in range(tb // per):
        slab = jnp.concatenate([y_s[c * per + u] for u in range(per)], axis=0).T
        cols = slice(c * LANES, (c + 1) * LANES)
        yf_ref[:, cols] = slab[0:Ch]
        yb_ref[:, cols] = slab[Ch:C]


def _scan(r, w_f, w_b, kd_f, kd_b, v, b_f, b_b, nkk, tb):
    Ch, SN = r.shape
    N = RWKV_HEAD
    nb = SN // (tb * N)
    fwd = pl.BlockSpec((Ch, tb * N), lambda i: (0, i))
    bwd = pl.BlockSpec((Ch, tb * N), lambda i: (0, nb - 1 - i))
    step_major = pltpu.VMEM((tb, N, 2 * Ch), F32)
    return pl.pallas_call(
        functools.partial(_scan_kernel, tb=tb),
        grid=(nb,),
        in_specs=[fwd, bwd] * 6,
        out_specs=(fwd, bwd),
        out_shape=(jax.ShapeDtypeStruct((Ch, SN), F32),) * 2,
        scratch_shapes=[step_major] * 7 + [pltpu.VMEM((N, N, 2 * Ch), F32),
                                           pltpu.VMEM((N, 2 * Ch), F32)],
        compiler_params=pltpu.CompilerParams(
            dimension_semantics=("arbitrary",), vmem_limit_bytes=VMEM_LIMIT),
        name="rwkv7_scan",
    )(r, r, w_f, w_b, kd_f, kd_b, v, v, b_f, b_b, nkk, nkk)


def _mixout_kernel(x_ref, o_ref, yf_ref, yb_ref, bonus_ref, g_ref, mod_ref, on_ref, gng_ref,
                   gnb_ref, seg_ref, wo_ref, out_ref):
    D = D_MODEL
    gt1 = mod_ref[0][:, 2 * D:3 * D]
    seg = seg_ref[...]
    y_mla = _rms(o_ref[0]) * on_ref[...]
    y = jnp.concatenate([yf_ref[0, hd] + yb_ref[0, hd] for hd in range(RWKV_HEADS)], axis=1)
    inv_n = 1.0 / RWKV_HEAD
    d = y - _seg_dot(y, seg) * inv_n
    var = _seg_dot(d * d, seg) * inv_n
    yn = d * lax.rsqrt(var + GN_EPS) * gng_ref[...] + gnb_ref[...]
    y_rwkv = (yn + bonus_ref[0]) * g_ref[0]
    proj = (jnp.dot(y_mla.astype(BF16), wo_ref[0:MLA_WIDTH, :], preferred_element_type=F32)
            + jnp.dot(y_rwkv.astype(BF16), wo_ref[MLA_WIDTH:, :], preferred_element_type=F32))
    out_ref[0] = x_ref[0] + gt1 * proj


def _mixout(x, o, y_f, y_b, bonus, g, mod_l, lw, tm):
    B, S, D = x.shape
    W = RWKV_WIDTH

    def full(a):
        return pl.BlockSpec(a.shape, lambda b, i: (0,) * a.ndim)

    weights = [lw["out_norm"], lw["gn_g"], lw["gn_b"], lw["seg"], lw["w_out"]]
    tokw = pl.BlockSpec((1, tm, W), lambda b, i: (b, i, 0))
    tokd = pl.BlockSpec((1, tm, D), lambda b, i: (b, i, 0))
    heads = pl.BlockSpec((1, RWKV_HEADS, tm, RWKV_HEAD), lambda b, i: (b, 0, i, 0))
    return pl.pallas_call(
        _mixout_kernel,
        grid=(B, S // tm),
        in_specs=[tokd, tokw, heads, heads, tokw, tokw,
                  pl.BlockSpec((1, 1, 6 * D), lambda b, i: (b, 0, 0))] + [full(a) for a in weights],
        out_specs=tokd,
        out_shape=jax.ShapeDtypeStruct((B, S, D), F32),
        compiler_params=pltpu.CompilerParams(
            dimension_semantics=("arbitrary", "arbitrary"), vmem_limit_bytes=VMEM_LIMIT),
        name="mixer_out",
    )(x, o, y_f, y_b, bonus, g, mod_l, *weights)


def _ffn_kernel(xm_ref, xp_ref, xn_ref, mod_ref, wg_ref, wv_ref, cw_ref, cb_ref, wd_ref,
                out_ref, h_ref, acc_ref, *, tm, nt, nc):
    i = pl.program_id(1)
    c = pl.program_id(2)
    D = D_MODEL
    te = tm + 2 * HALO
    mod = mod_ref[0]

    @pl.when(c == 0)
    def _():
        x_ext = jnp.concatenate([xp_ref[0], xm_ref[0], xn_ref[0]], axis=0)
        sh2 = mod[:, 3 * D:4 * D]
        sc2 = mod[:, 4 * D:5 * D]
        h_ref[...] = (_rms(x_ext) * (1.0 + sc2) + sh2).astype(BF16)
        acc_ref[...] = jnp.zeros_like(acc_ref)

    h = h_ref[...]
    gate = jnp.dot(h, wg_ref[...], preferred_element_type=F32)
    val = jnp.dot(h[HALO:HALO + tm], wv_ref[...], preferred_element_type=F32)
    row = lax.broadcasted_iota(jnp.int32, (te, 1), 0)
    lo, hi = _halo_bounds(i, nt, tm)
    gate = jnp.where((row >= lo) & (row < hi), gate, 0.0)
    gp = pltpu.roll(gate, 1, 0)[HALO:HALO + tm]
    gn = pltpu.roll(gate, te - 1, 0)[HALO:HALO + tm]
    gc = gate[HALO:HALO + tm]
    cw = cw_ref[...]
    gg = cw[0:1] * gp + cw[1:2] * gc + cw[2:3] * gn + cb_ref[...]
    act = gg * _sigmoid(gg) * val
    acc_ref[...] += jnp.dot(act.astype(BF16), wd_ref[...], preferred_element_type=F32)

    @pl.when(c == nc - 1)
    def _():
        gt2 = mod[:, 5 * D:6 * D]
        out_ref[0] = xm_ref[0] + gt2 * acc_ref[...]


def _ffn(x, mod_l, lw, tm, fc):
    B, S, D = x.shape
    nt = S // tm
    nc = D_FF // fc
    hb = tm // HALO
    nhb = S // HALO
    return pl.pallas_call(
        functools.partial(_ffn_kernel, tm=tm, nt=nt, nc=nc),
        grid=(B, nt, nc),
        in_specs=[
            pl.BlockSpec((1, tm, D), lambda b, i, c: (b, i, 0)),
            pl.BlockSpec((1, HALO, D), lambda b, i, c: (b, jnp.maximum(i * hb - 1, 0), 0)),
            pl.BlockSpec((1, HALO, D), lambda b, i, c: (b, jnp.minimum((i + 1) * hb, nhb - 1), 0)),
            pl.BlockSpec((1, 1, 6 * D), lambda b, i, c: (b, 0, 0)),
            pl.BlockSpec((D, fc), lambda b, i, c: (0, c)),
            pl.BlockSpec((D, fc), lambda b, i, c: (0, nc + c)),
            pl.BlockSpec((3, fc), lambda b, i, c: (0, c)),
            pl.BlockSpec((1, fc), lambda b, i, c: (0, c)),
            pl.BlockSpec((fc, D), lambda b, i, c: (c, 0)),
        ],
        out_specs=pl.BlockSpec((1, tm, D), lambda b, i, c: (b, i, 0)),
        out_shape=jax.ShapeDtypeStruct((B, S, D), F32),
        scratch_shapes=[pltpu.VMEM((tm + 2 * HALO, D), BF16), pltpu.VMEM((tm, D), F32)],
        compiler_params=pltpu.CompilerParams(
            dimension_semantics=("arbitrary", "arbitrary", "arbitrary"),
            vmem_limit_bytes=VMEM_LIMIT),
        name="conv_ffn",
    )(x, x, x, mod_l, lw["w_up"], lw["w_up"], lw["conv_w"], lw["conv_b"], lw["w_down"])


def _final_norm_kernel(x_ref, g_ref, o_ref):
    o_ref[0] = _rms(x_ref[0]) * g_ref[...]


def _final_norm(x, gain, tm):
    B, S, D = x.shape
    tok = pl.BlockSpec((1, tm, D), lambda b, i: (b, i, 0))
    return pl.pallas_call(
        _final_norm_kernel,
        grid=(B, S // tm),
        in_specs=[tok, pl.BlockSpec((1, D), lambda b, i: (0, 0))],
        out_specs=tok,
        out_shape=jax.ShapeDtypeStruct((B, S, D), F32),
        compiler_params=pltpu.CompilerParams(
            dimension_semantics=("arbitrary", "arbitrary"), vmem_limit_bytes=VMEM_LIMIT),
        name="final_norm",
    )(x, gain.reshape(1, D))


def _block_diag2(a, b):
    za = jnp.zeros((a.shape[0], b.shape[1]), a.dtype)
    zb = jnp.zeros((b.shape[0], a.shape[1]), a.dtype)
    return jnp.concatenate([jnp.concatenate([a, za], axis=1),
                            jnp.concatenate([zb, b], axis=1)], axis=0)


def _swap_halves(cols):
    half = cols.shape[-1] // 2
    return jnp.concatenate([cols[..., half:], cols[..., :half]], axis=-1)


def _layer_weights(l, p):
    W = RWKV_WIDTH
    w_in = p["w_in"][l]
    mla = w_in[:, :448]
    rw = w_in[:, 448:]
    kpe = mla[:, 384:448]
    rw_perm = jnp.concatenate([rw[:, :3 * W], rw[:, 3 * W + 256:], rw[:, 3 * W:3 * W + 256]], axis=1)
    mu = p["rwkv_mu"][l]
    mu_perm = jnp.concatenate([mu[:, :3 * W], mu[:, 3 * W + 256:], mu[:, 3 * W:3 * W + 256]], axis=1)
    w_in_perm = jnp.concatenate([mla, _swap_halves(kpe), rw_perm], axis=1).astype(BF16)

    w_uq = p["mla_w_uq"][l].reshape(MLA_Q_RANK, MLA_HEADS, MLA_QK)
    pe = w_uq[:, :, MLA_NOPE:]
    w_uq_t = jnp.concatenate([w_uq, _swap_halves(pe)], axis=-1).reshape(
        MLA_Q_RANK, MLA_HEADS * Q_HEAD_ROWS).T.astype(BF16)
    w_ukv = p["mla_w_ukv"][l].reshape(MLA_KV_RANK, MLA_HEADS, MLA_NOPE + MLA_V)
    w_uk = w_ukv[:, :, :MLA_NOPE].reshape(MLA_KV_RANK, MLA_HEADS * MLA_NOPE).astype(BF16)
    w_uv_t = w_ukv[:, :, MLA_NOPE:].reshape(MLA_KV_RANK, MLA_HEADS * MLA_V).T.astype(BF16)

    head = jnp.arange(W) // RWKV_HEAD
    seg = (head[:, None] == head[None, :]).astype(BF16)
    return {
        "w_in": w_in_perm,
        "mu": mu_perm,
        "q_norm": p["mla_q_norm"][l].reshape(1, -1),
        "w_uq_t": w_uq_t,
        "kv_norm": p["mla_kv_norm"][l].reshape(1, -1),
        "w_uk": w_uk,
        "w_uv_t": w_uv_t,
        "out_norm": p["mla_out_norm"][l].reshape(1, -1),
        "w0": p["rwkv_w0"][l].reshape(1, 2 * W),
        "w2": _block_diag2(p["rwkv_w2"][l, 0], p["rwkv_w2"][l, 1]).astype(BF16),
        "a0": p["rwkv_a0"][l].reshape(1, 2 * W),
        "a2": _block_diag2(p["rwkv_a2"][l, 0], p["rwkv_a2"][l, 1]).astype(BF16),
        "g2": p["rwkv_g2"][l].astype(BF16),
        "k_k": p["rwkv_k_k"][l].reshape(1, W),
        "k_a": p["rwkv_k_a"][l].reshape(1, W),
        "r_k": p["rwkv_r_k"][l].reshape(1, W),
        "gn_g": p["rwkv_gn_g"][l].reshape(1, W),
        "gn_b": p["rwkv_gn_b"][l].reshape(1, W),
        "seg": seg,
        "w_out": p["w_out"][l].astype(BF16),
        "w_up": p["ffn_w_up"][l].astype(BF16),
        "conv_w": p["ffn_conv_w"][l],
        "conv_b": p["ffn_conv_b"][l].reshape(1, D_FF),
        "w_down": p["ffn_w_down"][l].astype(BF16),
    }


def _forward(x, c, positions, p, cfg):
    B, S, D = x.shape
    chains = B * RWKV_HEADS
    mod = _adaln(c, p["ada_w"], p["ada_b"])
    rope_t, rope = _rope_tables(positions, cfg["tm_in"])
    for l in range(DEPTH):
        lw = _layer_weights(l, p)
        mod_l = mod[l].reshape(B, 1, 6 * D)
        (q_t, k, v_t, r, vv, nkk, w_f, w_b, kd_f, kd_b, b_f, b_b, bonus, g) = _inproj(
            x, mod_l, rope, rope_t, lw, cfg["tm_in"])
        o = _attention(q_t, k, v_t, cfg["tq"], cfg["tk"])

        def cm(a):
            return a.reshape(chains, S * RWKV_HEAD)

        y_f, y_b = _scan(cm(r), cm(w_f), cm(w_b), cm(kd_f), cm(kd_b), cm(vv), cm(b_f), cm(b_b),
                         cm(nkk), cfg["tb"])
        y_f = y_f.reshape(B, RWKV_HEADS, S, RWKV_HEAD)
        y_b = y_b.reshape(B, RWKV_HEADS, S, RWKV_HEAD)
        x = _mixout(x, o, y_f, y_b, bonus, g, mod_l, lw, cfg["tm_out"])
        x = _ffn(x, mod_l, lw, cfg["tm_ffn"], cfg["fc"])
    return _final_norm(x, p["final_norm"], cfg["tm_out"])


_CFG = {"tm_in": 256, "tq": 512, "tk": 512, "tb": 32, "tm_out": 512, "tm_ffn": 512, "fc": 1408}


def kernel(x, c, positions, ada_w, ada_b, w_in, mla_q_norm, mla_w_uq, mla_kv_norm, mla_w_ukv, mla_out_norm, rwkv_mu, rwkv_w0, rwkv_w2, rwkv_a0, rwkv_a2, rwkv_g2, rwkv_k_k, rwkv_k_a, rwkv_r_k, rwkv_gn_g, rwkv_gn_b, w_out, ffn_w_up, ffn_conv_w, ffn_conv_b, ffn_w_down, final_norm):
    p = dict(ada_w=ada_w, ada_b=ada_b, w_in=w_in, mla_q_norm=mla_q_norm, mla_w_uq=mla_w_uq,
             mla_kv_norm=mla_kv_norm, mla_w_ukv=mla_w_ukv, mla_out_norm=mla_out_norm,
             rwkv_mu=rwkv_mu, rwkv_w0=rwkv_w0, rwkv_w2=rwkv_w2, rwkv_a0=rwkv_a0, rwkv_a2=rwkv_a2,
             rwkv_g2=rwkv_g2, rwkv_k_k=rwkv_k_k, rwkv_k_a=rwkv_k_a, rwkv_r_k=rwkv_r_k,
             rwkv_gn_g=rwkv_gn_g, rwkv_gn_b=rwkv_gn_b, w_out=w_out, ffn_w_up=ffn_w_up,
             ffn_conv_w=ffn_conv_w, ffn_conv_b=ffn_conv_b, ffn_w_down=ffn_w_down,
             final_norm=final_norm)
    return _forward(x, c, positions, p, _CFG)
```

```python
import functools

import jax
import jax.numpy as jnp
from jax import lax
from jax.experimental import pallas as pl
from jax.experimental.pallas import tpu as pltpu

D_MODEL = 1024
DEPTH = 4
MLA_HEADS = 4
MLA_NOPE = 128
MLA_ROPE = 64
MLA_V = 128
MLA_Q_RANK = 256
MLA_KV_RANK = 128
MLA_WIDTH = MLA_HEADS * MLA_V
MLA_QK = MLA_NOPE + MLA_ROPE
ROPE_BASE = 10000.0
RWKV_WIDTH = D_MODEL - MLA_WIDTH
RWKV_HEAD = 64
RWKV_HEADS = RWKV_WIDTH // RWKV_HEAD
DECAY_LORA = 64
AAA_LORA = 64
GATE_LORA = 128
GN_EPS = 64e-5
D_FF = 2816
EPS = 1e-6

MLA_COLS = 512
RWKV_COLS = 3 * RWKV_WIDTH + GATE_LORA + 2 * DECAY_LORA + 2 * AAA_LORA
IN_COLS = MLA_COLS + RWKV_COLS
Q_HEAD_ROWS = MLA_NOPE + 2 * MLA_ROPE
BF16_ROWS = 16
V_ROWS = MLA_V + BF16_ROWS

LANES = 128
HALO = 8
VMEM_LIMIT = 56 * 1024 * 1024
LOG2E = 1.4426950408889634

F32 = jnp.float32
BF16 = jnp.bfloat16
NT_DIMS = (((1,), (1,)), ((), ()))


def _sigmoid(x):
    return 1.0 / (1.0 + jnp.exp(-x))


def _rms(x, eps=EPS):
    return x * lax.rsqrt(jnp.mean(x * x, axis=-1, keepdims=True) + eps)


def _seg_dot(x, seg):
    hi = x.astype(BF16)
    lo = (x - hi.astype(F32)).astype(BF16)
    return (jnp.dot(hi, seg, preferred_element_type=F32)
            + jnp.dot(lo, seg, preferred_element_type=F32))


def _halo_bounds(i, nt, tm):
    lo = jnp.where(i > 0, 0, HALO)
    hi = jnp.where(i < nt - 1, tm + 2 * HALO, tm + HALO)
    return lo, hi


def _adaln_kernel(c_ref, w_ref, b_ref, o_ref):
    c = c_ref[...]
    ca = c * _sigmoid(c)
    o_ref[0] = jnp.dot(ca, w_ref[0], precision=lax.Precision.HIGHEST,
                       preferred_element_type=F32) + b_ref[0]


def _adaln(c, ada_w, ada_b):
    L, D, D6 = ada_w.shape
    B = c.shape[0]
    nj = D6 // D
    return pl.pallas_call(
        _adaln_kernel,
        grid=(L, nj),
        in_specs=[
            pl.BlockSpec((B, D), lambda l, j: (0, 0)),
            pl.BlockSpec((1, D, D), lambda l, j: (l, 0, j)),
            pl.BlockSpec((1, 1, D), lambda l, j: (l, 0, j)),
        ],
        out_specs=pl.BlockSpec((1, B, D), lambda l, j: (l, 0, j)),
        out_shape=jax.ShapeDtypeStruct((L, B, D6), F32),
        compiler_params=pltpu.CompilerParams(
            dimension_semantics=("arbitrary", "arbitrary"), vmem_limit_bytes=VMEM_LIMIT),
        name="adaln",
    )(c, ada_w, ada_b.reshape(L, 1, D6))


def _rope_kernel(pos_ref, freq_ref, tab_t_ref, tab_ref):
    ang = freq_ref[...] * pos_ref[0].astype(F32)
    row = lax.broadcasted_iota(jnp.int32, ang.shape, 0)
    cos = jnp.cos(ang)
    sin = jnp.sin(ang)
    tab_t = jnp.where(row < 64, cos, jnp.where(row < 96, -sin, sin))
    tab_t_ref[0] = tab_t
    tab_ref[0] = tab_t.T


def _rope_tables(positions, ts):
    B, S = positions.shape
    half = MLA_ROPE // 2
    inv_freq = ROPE_BASE ** (-jnp.arange(half, dtype=F32) / half)
    freq = jnp.tile(inv_freq, 4).reshape(LANES, 1)
    return pl.pallas_call(
        _rope_kernel,
        grid=(B, S // ts),
        in_specs=[
            pl.BlockSpec((1, 1, ts), lambda b, i: (b, 0, i)),
            pl.BlockSpec((LANES, 1), lambda b, i: (0, 0)),
        ],
        out_specs=(pl.BlockSpec((1, LANES, ts), lambda b, i: (b, 0, i)),
                   pl.BlockSpec((1, ts, LANES), lambda b, i: (b, i, 0))),
        out_shape=(jax.ShapeDtypeStruct((B, LANES, S), F32),
                   jax.ShapeDtypeStruct((B, S, LANES), F32)),
        compiler_params=pltpu.CompilerParams(
            dimension_semantics=("arbitrary", "arbitrary"), vmem_limit_bytes=VMEM_LIMIT),
        name="rope_tables",
    )(positions.reshape(B, 1, S), freq)


def _inproj_kernel(xm_ref, xp_ref, xn_ref, mod_ref, rope_ref, rope_t_ref, w_in_ref, mu_ref,
                   qn_ref, wuq_t_ref, kvn_ref, wuk_ref, wuv_t_ref,
                   w0_ref, w2_ref, a0_ref, a2_ref, g2_ref, kk_ref, ka_ref, rk_ref, seg_ref,
                   qt_out, k_out, vt_out, r_out, vv_out, nkk_out, wf_out, wb_out,
                   kdf_out, kdb_out, bf_out, bb_out, bonus_out, g_out, *, tm, nt):
    i = pl.program_id(1)
    D = D_MODEL
    W = RWKV_WIDTH
    te = tm + 2 * HALO
    x_ext = jnp.concatenate([xp_ref[0], xm_ref[0], xn_ref[0]], axis=0)
    mod = mod_ref[0]
    sh1 = mod[:, 0:D]
    sc1 = mod[:, D:2 * D]
    h = _rms(x_ext) * (1.0 + sc1) + sh1
    z = jnp.dot(h.astype(BF16), w_in_ref[...], preferred_element_type=F32)

    zr = z[:, MLA_COLS:]
    row = lax.broadcasted_iota(jnp.int32, (te, 1), 0)
    lo, hi = _halo_bounds(i, nt, tm)
    zr = jnp.where((row >= lo) & (row < hi), zr, 0.0)
    zc = zr[HALO:HALO + tm]
    zp = pltpu.roll(zr, 1, 0)[HALO:HALO + tm]
    zn = pltpu.roll(zr, te - 1, 0)[HALO:HALO + tm]
    mu = mu_ref[...]
    zs = zc + mu[0:1] * (zp - zc) + mu[1:2] * (zn - zc)

    zm = z[HALO:HALO + tm, 0:MLA_COLS]
    hq = (_rms(zm[:, 0:MLA_Q_RANK]) * qn_ref[...]).astype(BF16)
    q_t = lax.dot_general(wuq_t_ref[...], hq, NT_DIMS, preferred_element_type=F32)
    hkv = (_rms(zm[:, MLA_Q_RANK:MLA_Q_RANK + MLA_KV_RANK]) * kvn_ref[...]).astype(BF16)
    kn = jnp.dot(hkv, wuk_ref[...], preferred_element_type=F32)
    v_t = lax.dot_general(wuv_t_ref[...], hkv, NT_DIMS, preferred_element_type=F32)
    kprod = zm[:, 384:512] * rope_ref[0]
    kpe = (kprod + pltpu.roll(kprod, 64, 1))[:, 0:MLA_ROPE].astype(BF16)
    rope_t = rope_t_ref[0]
    qscale = (MLA_QK ** -0.5) * LOG2E
    for hd in range(MLA_HEADS):
        qh = q_t[hd * Q_HEAD_ROWS:(hd + 1) * Q_HEAD_ROWS]
        qprod = qh[MLA_NOPE:] * rope_t
        qpe = qprod[0:MLA_ROPE] + qprod[MLA_ROPE:]
        qt_out[0, hd, 0:MLA_NOPE, :] = (qh[0:MLA_NOPE] * qscale).astype(BF16)
        qt_out[0, hd, MLA_NOPE:MLA_QK, :] = (qpe * qscale).astype(BF16)
        k_out[0, hd, :, 0:MLA_NOPE] = kn[:, hd * MLA_NOPE:(hd + 1) * MLA_NOPE].astype(BF16)
        k_out[0, hd, :, MLA_NOPE:MLA_QK] = kpe
        vt_out[0, hd, 0:MLA_V, :] = v_t[hd * MLA_V:(hd + 1) * MLA_V].astype(BF16)
        vt_out[0, hd, MLA_V:V_ROWS, :] = jnp.ones((BF16_ROWS, tm), BF16)

    r = zs[:, 0:W]
    k = zs[:, W:2 * W]
    v = zs[:, 2 * W:3 * W]
    xg = zs[:, 3 * W:3 * W + GATE_LORA]
    xw = zs[:, 3 * W + GATE_LORA:3 * W + GATE_LORA + 2 * DECAY_LORA]
    xa = zs[:, 3 * W + GATE_LORA + 2 * DECAY_LORA:]
    seg = seg_ref[...]
    g_out[0] = jnp.dot(_sigmoid(xg).astype(BF16), g2_ref[...], preferred_element_type=F32)
    kk = k * kk_ref[...]
    kkn = kk * lax.rsqrt(_seg_dot(kk * kk, seg) + 1e-12)
    lw = jnp.dot(jnp.tanh(xw).astype(BF16), w2_ref[...], preferred_element_type=F32) + w0_ref[...]
    wl = jnp.minimum(lw, 0.0) - jnp.log(1.0 + jnp.exp(-jnp.abs(lw))) - 0.5
    w = jnp.exp(-jnp.exp(wl))
    a = _sigmoid(jnp.dot(xa.astype(BF16), a2_ref[...], preferred_element_type=F32) + a0_ref[...])
    ka = ka_ref[...]
    kd_f = k * (1.0 + (a[:, 0:W] - 1.0) * ka)
    kd_b = k * (1.0 + (a[:, W:] - 1.0) * ka)
    r_out[0] = r
    vv_out[0] = v
    nkk_out[0] = -kkn
    wf_out[0] = w[:, 0:W]
    wb_out[0] = w[:, W:]
    kdf_out[0] = kd_f
    kdb_out[0] = kd_b
    bf_out[0] = kkn * a[:, 0:W]
    bb_out[0] = kkn * a[:, W:]
    bonus_out[0] = _seg_dot(r * (0.5 * (kd_f + kd_b)) * rk_ref[...], seg) * v


def _inproj(x, mod_l, rope, rope_t, lw, tm):
    B, S, D = x.shape
    nt = S // tm
    hb = tm // HALO
    nhb = S // HALO
    W = RWKV_WIDTH
    H = MLA_HEADS

    def full(a):
        return pl.BlockSpec(a.shape, lambda b, i: (0,) * a.ndim)

    weights = [lw["w_in"], lw["mu"], lw["q_norm"], lw["w_uq_t"], lw["kv_norm"], lw["w_uk"],
               lw["w_uv_t"], lw["w0"], lw["w2"], lw["a0"], lw["a2"], lw["g2"], lw["k_k"],
               lw["k_a"], lw["r_k"], lw["seg"]]
    tok = pl.BlockSpec((1, tm, W), lambda b, i: (b, i, 0))
    tok_shape = jax.ShapeDtypeStruct((B, S, W), F32)
    out_shape = (
        jax.ShapeDtypeStruct((B, H, MLA_QK, S), BF16),
        jax.ShapeDtypeStruct((B, H, S, MLA_QK), BF16),
        jax.ShapeDtypeStruct((B, H, V_ROWS, S), BF16),
    ) + (tok_shape,) * 11
    out_specs = (
        pl.BlockSpec((1, H, MLA_QK, tm), lambda b, i: (b, 0, 0, i)),
        pl.BlockSpec((1, H, tm, MLA_QK), lambda b, i: (b, 0, i, 0)),
        pl.BlockSpec((1, H, V_ROWS, tm), lambda b, i: (b, 0, 0, i)),
    ) + (tok,) * 11
    return pl.pallas_call(
        functools.partial(_inproj_kernel, tm=tm, nt=nt),
        grid=(B, nt),
        in_specs=[
            pl.BlockSpec((1, tm, D), lambda b, i: (b, i, 0)),
            pl.BlockSpec((1, HALO, D), lambda b, i: (b, jnp.maximum(i * hb - 1, 0), 0)),
            pl.BlockSpec((1, HALO, D), lambda b, i: (b, jnp.minimum((i + 1) * hb, nhb - 1), 0)),
            pl.BlockSpec((1, 1, 6 * D), lambda b, i: (b, 0, 0)),
            pl.BlockSpec((1, tm, LANES), lambda b, i: (b, i, 0)),
            pl.BlockSpec((1, LANES, tm), lambda b, i: (b, 0, i)),
        ] + [full(a) for a in weights],
        out_specs=out_specs,
        out_shape=out_shape,
        compiler_params=pltpu.CompilerParams(
            dimension_semantics=("arbitrary", "arbitrary"), vmem_limit_bytes=VMEM_LIMIT),
        name="inproj",
    )(x, x, x, mod_l, rope, rope_t, *weights)


def _attn_kernel(qt_ref, k_ref, vt_ref, o_ref, s_ref, acc_ref, m_ref, *, tk, nk):
    q_t = qt_ref[0, 0]

    def scores(j, slot):
        start = pl.multiple_of(j * tk, tk)
        s_ref[slot] = jnp.dot(k_ref[0, 0, pl.ds(start, tk), :], q_t,
                              preferred_element_type=F32)

    def consume(j, slot):
        start = pl.multiple_of(j * tk, tk)
        s_t = s_ref[slot]
        m = m_ref[...]
        m_new = jnp.maximum(m, jnp.max(s_t, axis=0, keepdims=True))
        p_t = jnp.exp2(s_t - m_new).astype(BF16)
        alpha = jnp.exp2(m - m_new)
        acc_ref[...] = acc_ref[...] * alpha + jnp.dot(
            vt_ref[0, 0, :, pl.ds(start, tk)], p_t, preferred_element_type=F32)
        m_ref[...] = m_new

    m_ref[...] = jnp.full(m_ref.shape, -jnp.inf, F32)
    acc_ref[...] = jnp.zeros_like(acc_ref)
    scores(0, 0)

    def body(i, _):
        j = 2 * i
        scores(j + 1, 1)
        consume(j, 0)
        scores(jnp.minimum(j + 2, nk - 1), 0)
        consume(j + 1, 1)
        return 0

    lax.fori_loop(0, nk // 2, body, 0)
    acc = acc_ref[...]
    o_ref[0] = (acc[0:MLA_V] / acc[MLA_V:MLA_V + 1]).T


def _attention(q_t, k, v_t, tq, tk):
    B, H, S, _ = k.shape
    nk = S // tk
    assert nk % 2 == 0
    return pl.pallas_call(
        functools.partial(_attn_kernel, tk=tk, nk=nk),
        grid=(B, H, S // tq),
        in_specs=[
            pl.BlockSpec((1, 1, MLA_QK, tq), lambda b, h, i: (b, h, 0, i)),
            pl.BlockSpec((1, 1, S, MLA_QK), lambda b, h, i: (b, h, 0, 0)),
            pl.BlockSpec((1, 1, V_ROWS, S), lambda b, h, i: (b, h, 0, 0)),
        ],
        out_specs=pl.BlockSpec((1, tq, MLA_V), lambda b, h, i: (b, i, h)),
        out_shape=jax.ShapeDtypeStruct((B, S, H * MLA_V), F32),
        scratch_shapes=[pltpu.VMEM((2, tk, tq), F32), pltpu.VMEM((V_ROWS, tq), F32),
                        pltpu.VMEM((1, tq), F32)],
        compiler_params=pltpu.CompilerParams(
            dimension_semantics=("arbitrary", "arbitrary", "arbitrary"),
            vmem_limit_bytes=VMEM_LIMIT),
        name="mla_attention",
    )(q_t, k, v_t)


SCAN_RING = 4
_R, _W, _KD, _V, _B, _NK = range(6)


def _scan_kernel(rf_ref, rb_ref, wf_ref, wb_ref, kdf_ref, kdb_ref, vf_ref, vb_ref, bf_ref, bb_ref,
                 nkf_ref, nkb_ref, yf_ref, yb_ref, *scratch, tb):
    N = RWKV_HEAD
    HP = RWKV_HEADS // 2
    B = rf_ref.shape[0]
    G = HP * B
    slabs = scratch[0:SCAN_RING]
    y_ring = scratch[SCAN_RING:2 * SCAN_RING]
    state_ref, sa_ref = scratch[2 * SCAN_RING:]
    C = sa_ref.shape[-1]
    pairs = ((rf_ref, rb_ref), (wf_ref, wb_ref), (kdf_ref, kdb_ref), (vf_ref, vb_ref),
             (bf_ref, bb_ref), (nkf_ref, nkb_ref))

    def fill_one(slot, j, idx):
        f_ref, b_ref = pairs[idx]
        f_row = f_ref[:, j, :]
        b_row = b_ref[:, tb - 1 - j, :]
        rows = ([f_row[:, hp * LANES:(hp + 1) * LANES] for hp in range(HP)]
                + [b_row[:, hp * LANES:(hp + 1) * LANES] for hp in range(HP)])
        t = jnp.concatenate(rows, axis=0).T
        slabs[slot][idx] = jnp.concatenate([t[0:N], t[N:2 * N]], axis=1)

    def fill(slot, j):
        for idx in range(len(pairs)):
            fill_one(slot, j, idx)

    def emit(slot, j):
        y = y_ring[slot][...]
        t = jnp.concatenate([y[:, 0:2 * G], y[:, 2 * G:C]], axis=0).T
        yf_ref[:, j, :] = jnp.concatenate([t[hp * B:(hp + 1) * B] for hp in range(HP)], axis=1)
        yb_ref[:, tb - 1 - j, :] = jnp.concatenate(
            [t[G + hp * B:G + (hp + 1) * B] for hp in range(HP)], axis=1)

    @pl.when(pl.program_id(0) == 0)
    def _():
        state_ref[...] = jnp.zeros_like(state_ref)
        for y_ref in y_ring:
            y_ref[...] = jnp.zeros_like(y_ref)

    fill(0, 0)
    fill(1, 1)
    sa0 = jnp.zeros((N, C), F32)
    for kc in range(N):
        sa0 = sa0 + state_ref[kc] * slabs[0][_NK, pl.ds(kc, 1), :]
    sa_ref[...] = sa0

    def step(j, cur):
        nxt = (cur + 1) % SCAN_RING
        fill((cur + 2) % SCAN_RING, jnp.minimum(j + 2, tb - 1))
        emit((cur + SCAN_RING - 1) % SCAN_RING, jnp.maximum(j - 1, 0))
        for hv in range(2):
            rows = slice(hv * (N // 2), (hv + 1) * (N // 2))
            sa = sa_ref[rows, :]
            vv = slabs[cur][_V, rows, :]
            y = jnp.zeros_like(sa)
            sa_next = jnp.zeros_like(sa)
            for kc in range(N):
                row = pl.ds(kc, 1)
                s_new = (state_ref[kc, rows, :] * slabs[cur][_W, row, :]
                         + sa * slabs[cur][_B, row, :] + vv * slabs[cur][_KD, row, :])
                state_ref[kc, rows, :] = s_new
                y = y + s_new * slabs[cur][_R, row, :]
                sa_next = sa_next + s_new * slabs[nxt][_NK, row, :]
            sa_ref[rows, :] = sa_next
            y_ring[cur][rows, :] = y

    def body(i, _):
        for u in range(SCAN_RING):
            pl.when(i >= -u)(functools.partial(step, i * SCAN_RING + u, u))
        return 0

    lax.fori_loop(0, tb // SCAN_RING, body, 0)
    emit((tb - 1) % SCAN_RING, tb - 1)


def _scan(r, w_f, w_b, kd_f, kd_b, v, b_f, b_b, nkk, tb):
    B, S, W = r.shape
    nb = S // tb
    assert tb % SCAN_RING == 0
    chains = 2 * RWKV_HEADS * B
    fwd = pl.BlockSpec((B, tb, W), lambda i: (0, i, 0))
    bwd = pl.BlockSpec((B, tb, W), lambda i: (0, nb - 1 - i, 0))
    return pl.pallas_call(
        functools.partial(_scan_kernel, tb=tb),
        grid=(nb,),
        in_specs=[fwd, bwd] * 6,
        out_specs=(fwd, bwd),
        out_shape=(jax.ShapeDtypeStruct((B, S, W), F32),) * 2,
        scratch_shapes=[pltpu.VMEM((6, RWKV_HEAD, chains), F32)] * SCAN_RING
        + [pltpu.VMEM((RWKV_HEAD, chains), F32)] * SCAN_RING
        + [pltpu.VMEM((RWKV_HEAD, RWKV_HEAD, chains), F32),
                        pltpu.VMEM((RWKV_HEAD, chains), F32)],
        compiler_params=pltpu.CompilerParams(
            dimension_semantics=("arbitrary",), vmem_limit_bytes=VMEM_LIMIT),
        name="rwkv7_scan",
    )(r, r, w_f, w_b, kd_f, kd_b, v, v, b_f, b_b, nkk, nkk)


def _mixout_kernel(x_ref, o_ref, yf_ref, yb_ref, bonus_ref, g_ref, mod_ref, on_ref, gng_ref,
                   gnb_ref, seg_ref, wo_ref, out_ref):
    D = D_MODEL
    gt1 = mod_ref[0][:, 2 * D:3 * D]
    seg = seg_ref[...]
    y_mla = _rms(o_ref[0]) * on_ref[...]
    y = yf_ref[0] + yb_ref[0]
    inv_n = 1.0 / RWKV_HEAD
    d = y - _seg_dot(y, seg) * inv_n
    var = _seg_dot(d * d, seg) * inv_n
    yn = d * lax.rsqrt(var + GN_EPS) * gng_ref[...] + gnb_ref[...]
    y_rwkv = (yn + bonus_ref[0]) * g_ref[0]
    proj = (jnp.dot(y_mla.astype(BF16), wo_ref[0:MLA_WIDTH, :], preferred_element_type=F32)
            + jnp.dot(y_rwkv.astype(BF16), wo_ref[MLA_WIDTH:, :], preferred_element_type=F32))
    out_ref[0] = x_ref[0] + gt1 * proj


def _mixout(x, o, y_f, y_b, bonus, g, mod_l, lw, tm):
    B, S, D = x.shape
    W = RWKV_WIDTH

    def full(a):
        return pl.BlockSpec(a.shape, lambda b, i: (0,) * a.ndim)

    weights = [lw["out_norm"], lw["gn_g"], lw["gn_b"], lw["seg"], lw["w_out"]]
    tokw = pl.BlockSpec((1, tm, W), lambda b, i: (b, i, 0))
    tokd = pl.BlockSpec((1, tm, D), lambda b, i: (b, i, 0))
    return pl.pallas_call(
        _mixout_kernel,
        grid=(B, S // tm),
        in_specs=[tokd, tokw, tokw, tokw, tokw, tokw,
                  pl.BlockSpec((1, 1, 6 * D), lambda b, i: (b, 0, 0))] + [full(a) for a in weights],
        out_specs=tokd,
        out_shape=jax.ShapeDtypeStruct((B, S, D), F32),
        compiler_params=pltpu.CompilerParams(
            dimension_semantics=("arbitrary", "arbitrary"), vmem_limit_bytes=VMEM_LIMIT),
        name="mixer_out",
    )(x, o, y_f, y_b, bonus, g, mod_l, *weights)


def _ffn_kernel(xm_ref, xp_ref, xn_ref, mod_ref, wg_ref, wv_ref, cw_ref, cb_ref, wd_ref,
                out_ref, h_ref, acc_ref, *, tm, nt, nc):
    i = pl.program_id(1)
    c = pl.program_id(2)
    D = D_MODEL
    te = tm + 2 * HALO
    mod = mod_ref[0]

    @pl.when(c == 0)
    def _():
        x_ext = jnp.concatenate([xp_ref[0], xm_ref[0], xn_ref[0]], axis=0)
        sh2 = mod[:, 3 * D:4 * D]
        sc2 = mod[:, 4 * D:5 * D]
        h_ref[...] = (_rms(x_ext) * (1.0 + sc2) + sh2).astype(BF16)
        acc_ref[...] = jnp.zeros_like(acc_ref)

    h = h_ref[...]
    gate = jnp.dot(h, wg_ref[...], preferred_element_type=F32)
    val = jnp.dot(h[HALO:HALO + tm], wv_ref[...], preferred_element_type=F32)
    row = lax.broadcasted_iota(jnp.int32, (te, 1), 0)
    lo, hi = _halo_bounds(i, nt, tm)
    gate = jnp.where((row >= lo) & (row < hi), gate, 0.0)
    gp = pltpu.roll(gate, 1, 0)[HALO:HALO + tm]
    gn = pltpu.roll(gate, te - 1, 0)[HALO:HALO + tm]
    gc = gate[HALO:HALO + tm]
    cw = cw_ref[...]
    gg = cw[0:1] * gp + cw[1:2] * gc + cw[2:3] * gn + cb_ref[...]
    act = gg * _sigmoid(gg) * val
    acc_ref[...] += jnp.dot(act.astype(BF16), wd_ref[...], preferred_element_type=F32)

    @pl.when(c == nc - 1)
    def _():
        gt2 = mod[:, 5 * D:6 * D]
        out_ref[0] = xm_ref[0] + gt2 * acc_ref[...]


def _ffn(x, mod_l, lw, tm, fc):
    B, S, D = x.shape
    nt = S // tm
    nc = D_FF // fc
    hb = tm // HALO
    nhb = S // HALO
    return pl.pallas_call(
        functools.partial(_ffn_kernel, tm=tm, nt=nt, nc=nc),
        grid=(B, nt, nc),
        in_specs=[
            pl.BlockSpec((1, tm, D), lambda b, i, c: (b, i, 0)),
            pl.BlockSpec((1, HALO, D), lambda b, i, c: (b, jnp.maximum(i * hb - 1, 0), 0)),
            pl.BlockSpec((1, HALO, D), lambda b, i, c: (b, jnp.minimum((i + 1) * hb, nhb - 1), 0)),
            pl.BlockSpec((1, 1, 6 * D), lambda b, i, c: (b, 0, 0)),
            pl.BlockSpec((D, fc), lambda b, i, c: (0, c)),
            pl.BlockSpec((D, fc), lambda b, i, c: (0, nc + c)),
            pl.BlockSpec((3, fc), lambda b, i, c: (0, c)),
            pl.BlockSpec((1, fc), lambda b, i, c: (0, c)),
            pl.BlockSpec((fc, D), lambda b, i, c: (c, 0)),
        ],
        out_specs=pl.BlockSpec((1, tm, D), lambda b, i, c: (b, i, 0)),
        out_shape=jax.ShapeDtypeStruct((B, S, D), F32),
        scratch_shapes=[pltpu.VMEM((tm + 2 * HALO, D), BF16), pltpu.VMEM((tm, D), F32)],
        compiler_params=pltpu.CompilerParams(
            dimension_semantics=("arbitrary", "arbitrary", "arbitrary"),
            vmem_limit_bytes=VMEM_LIMIT),
        name="conv_ffn",
    )(x, x, x, mod_l, lw["w_up"], lw["w_up"], lw["conv_w"], lw["conv_b"], lw["w_down"])


def _final_norm_kernel(x_ref, g_ref, o_ref):
    o_ref[0] = _rms(x_ref[0]) * g_ref[...]


def _final_norm(x, gain, tm):
    B, S, D = x.shape
    tok = pl.BlockSpec((1, tm, D), lambda b, i: (b, i, 0))
    return pl.pallas_call(
        _final_norm_kernel,
        grid=(B, S // tm),
        in_specs=[tok, pl.BlockSpec((1, D), lambda b, i: (0, 0))],
        out_specs=tok,
        out_shape=jax.ShapeDtypeStruct((B, S, D), F32),
        compiler_params=pltpu.CompilerParams(
            dimension_semantics=("arbitrary", "arbitrary"), vmem_limit_bytes=VMEM_LIMIT),
        name="final_norm",
    )(x, gain.reshape(1, D))


def _block_diag2(a, b):
    za = jnp.zeros((a.shape[0], b.shape[1]), a.dtype)
    zb = jnp.zeros((b.shape[0], a.shape[1]), a.dtype)
    return jnp.concatenate([jnp.concatenate([a, za], axis=1),
                            jnp.concatenate([zb, b], axis=1)], axis=0)


def _swap_halves(cols):
    half = cols.shape[-1] // 2
    return jnp.concatenate([cols[..., half:], cols[..., :half]], axis=-1)


def _layer_weights(l, p):
    W = RWKV_WIDTH
    w_in = p["w_in"][l]
    mla = w_in[:, :448]
    rw = w_in[:, 448:]
    kpe = mla[:, 384:448]
    rw_perm = jnp.concatenate([rw[:, :3 * W], rw[:, 3 * W + 256:], rw[:, 3 * W:3 * W + 256]], axis=1)
    mu = p["rwkv_mu"][l]
    mu_perm = jnp.concatenate([mu[:, :3 * W], mu[:, 3 * W + 256:], mu[:, 3 * W:3 * W + 256]], axis=1)
    w_in_perm = jnp.concatenate([mla, _swap_halves(kpe), rw_perm], axis=1).astype(BF16)

    w_uq = p["mla_w_uq"][l].reshape(MLA_Q_RANK, MLA_HEADS, MLA_QK)
    pe = w_uq[:, :, MLA_NOPE:]
    w_uq_t = jnp.concatenate([w_uq, _swap_halves(pe)], axis=-1).reshape(
        MLA_Q_RANK, MLA_HEADS * Q_HEAD_ROWS).T.astype(BF16)
    w_ukv = p["mla_w_ukv"][l].reshape(MLA_KV_RANK, MLA_HEADS, MLA_NOPE + MLA_V)
    w_uk = w_ukv[:, :, :MLA_NOPE].reshape(MLA_KV_RANK, MLA_HEADS * MLA_NOPE).astype(BF16)
    w_uv_t = w_ukv[:, :, MLA_NOPE:].reshape(MLA_KV_RANK, MLA_HEADS * MLA_V).T.astype(BF16)

    head = jnp.arange(W) // RWKV_HEAD
    seg = (head[:, None] == head[None, :]).astype(BF16)
    return {
        "w_in": w_in_perm,
        "mu": mu_perm,
        "q_norm": p["mla_q_norm"][l].reshape(1, -1),
        "w_uq_t": w_uq_t,
        "kv_norm": p["mla_kv_norm"][l].reshape(1, -1),
        "w_uk": w_uk,
        "w_uv_t": w_uv_t,
        "out_norm": p["mla_out_norm"][l].reshape(1, -1),
        "w0": p["rwkv_w0"][l].reshape(1, 2 * W),
        "w2": _block_diag2(p["rwkv_w2"][l, 0], p["rwkv_w2"][l, 1]).astype(BF16),
        "a0": p["rwkv_a0"][l].reshape(1, 2 * W),
        "a2": _block_diag2(p["rwkv_a2"][l, 0], p["rwkv_a2"][l, 1]).astype(BF16),
        "g2": p["rwkv_g2"][l].astype(BF16),
        "k_k": p["rwkv_k_k"][l].reshape(1, W),
        "k_a": p["rwkv_k_a"][l].reshape(1, W),
        "r_k": p["rwkv_r_k"][l].reshape(1, W),
        "gn_g": p["rwkv_gn_g"][l].reshape(1, W),
        "gn_b": p["rwkv_gn_b"][l].reshape(1, W),
        "seg": seg,
        "w_out": p["w_out"][l].astype(BF16),
        "w_up": p["ffn_w_up"][l].astype(BF16),
        "conv_w": p["ffn_conv_w"][l],
        "conv_b": p["ffn_conv_b"][l].reshape(1, D_FF),
        "w_down": p["ffn_w_down"][l].astype(BF16),
    }


def _forward(x, c, positions, p, cfg):
    B, S, D = x.shape
    mod = _adaln(c, p["ada_w"], p["ada_b"])
    rope_t, rope = _rope_tables(positions, cfg["tm_in"])
    for l in range(DEPTH):
        lw = _layer_weights(l, p)
        mod_l = mod[l].reshape(B, 1, 6 * D)
        (q_t, k, v_t, r, vv, nkk, w_f, w_b, kd_f, kd_b, b_f, b_b, bonus, g) = _inproj(
            x, mod_l, rope, rope_t, lw, cfg["tm_in"])
        o = _attention(q_t, k, v_t, cfg["tq"], cfg["tk"])
        y_f, y_b = _scan(r, w_f, w_b, kd_f, kd_b, vv, b_f, b_b, nkk, cfg["tb"])
        x = _mixout(x, o, y_f, y_b, bonus, g, mod_l, lw, cfg["tm_out"])
        x = _ffn(x, mod_l, lw, cfg["tm_ffn"], cfg["fc"])
    return _final_norm(x, p["final_norm"], cfg["tm_out"])


_CFG = {"tm_in": 256, "tq": 512, "tk": 512, "tb": 32, "tm_out": 512, "tm_ffn": 512, "fc": 1408}


def kernel(x, c, positions, ada_w, ada_b, w_in, mla_q_norm, mla_w_uq, mla_kv_norm, mla_w_ukv, mla_out_norm, rwkv_mu, rwkv_w0, rwkv_w2, rwkv_a0, rwkv_a2, rwkv_g2, rwkv_k_k, rwkv_k_a, rwkv_r_k, rwkv_gn_g, rwkv_gn_b, w_out, ffn_w_up, ffn_conv_w, ffn_conv_b, ffn_w_down, final_norm):
    p = dict(ada_w=ada_w, ada_b=ada_b, w_in=w_in, mla_q_norm=mla_q_norm, mla_w_uq=mla_w_uq,
             mla_kv_norm=mla_kv_norm, mla_w_ukv=mla_w_ukv, mla_out_norm=mla_out_norm,
             rwkv_mu=rwkv_mu, rwkv_w0=rwkv_w0, rwkv_w2=rwkv_w2, rwkv_a0=rwkv_a0, rwkv_a2=rwkv_a2,
             rwkv_g2=rwkv_g2, rwkv_k_k=rwkv_k_k, rwkv_k_a=rwkv_k_a, rwkv_r_k=rwkv_r_k,
             rwkv_gn_g=rwkv_gn_g, rwkv_gn_b=rwkv_gn_b, w_out=w_out, ffn_w_up=ffn_w_up,
             ffn_conv_w=ffn_conv_w, ffn_conv_b=ffn_conv_b, ffn_w_down=ffn_w_down,
             final_norm=final_norm)
    return _forward(x, c, positions, p, _CFG)
```

```python
import functools

import jax
import jax.numpy as jnp
from jax import lax
from jax.experimental import pallas as pl
from jax.experimental.pallas import tpu as pltpu

D_MODEL = 1024
DEPTH = 4
MLA_HEADS = 4
MLA_NOPE = 128
MLA_ROPE = 64
MLA_V = 128
MLA_Q_RANK = 256
MLA_KV_RANK = 128
MLA_WIDTH = MLA_HEADS * MLA_V
MLA_QK = MLA_NOPE + MLA_ROPE
ROPE_BASE = 10000.0
RWKV_WIDTH = D_MODEL - MLA_WIDTH
RWKV_HEAD = 64
RWKV_HEADS = RWKV_WIDTH // RWKV_HEAD
DECAY_LORA = 64
AAA_LORA = 64
GATE_LORA = 128
GN_EPS = 64e-5
D_FF = 2816
EPS = 1e-6

MLA_COLS = 512
RWKV_COLS = 3 * RWKV_WIDTH + GATE_LORA + 2 * DECAY_LORA + 2 * AAA_LORA
IN_COLS = MLA_COLS + RWKV_COLS
Q_HEAD_ROWS = MLA_NOPE + 2 * MLA_ROPE
BF16_ROWS = 16
V_ROWS = MLA_V + BF16_ROWS

LANES = 128
HALO = 8
VMEM_LIMIT = 56 * 1024 * 1024
LOG2E = 1.4426950408889634

F32 = jnp.float32
BF16 = jnp.bfloat16
NT_DIMS = (((1,), (1,)), ((), ()))


def _sigmoid(x):
    return 1.0 / (1.0 + jnp.exp(-x))


def _rms(x, eps=EPS):
    return x * lax.rsqrt(jnp.mean(x * x, axis=-1, keepdims=True) + eps)


def _seg_dot(x, seg):
    hi = x.astype(BF16)
    lo = (x - hi.astype(F32)).astype(BF16)
    return (jnp.dot(hi, seg, preferred_element_type=F32)
            + jnp.dot(lo, seg, preferred_element_type=F32))


def _halo_bounds(i, nt, tm):
    lo = jnp.where(i > 0, 0, HALO)
    hi = jnp.where(i < nt - 1, tm + 2 * HALO, tm + HALO)
    return lo, hi


def _adaln_kernel(c_ref, w_ref, b_ref, o_ref):
    c = c_ref[...]
    ca = c * _sigmoid(c)
    o_ref[0] = jnp.dot(ca, w_ref[0], precision=lax.Precision.HIGHEST,
                       preferred_element_type=F32) + b_ref[0]


def _adaln(c, ada_w, ada_b):
    L, D, D6 = ada_w.shape
    B = c.shape[0]
    nj = D6 // D
    return pl.pallas_call(
        _adaln_kernel,
        grid=(L, nj),
        in_specs=[
            pl.BlockSpec((B, D), lambda l, j: (0, 0)),
            pl.BlockSpec((1, D, D), lambda l, j: (l, 0, j)),
            pl.BlockSpec((1, 1, D), lambda l, j: (l, 0, j)),
        ],
        out_specs=pl.BlockSpec((1, B, D), lambda l, j: (l, 0, j)),
        out_shape=jax.ShapeDtypeStruct((L, B, D6), F32),
        compiler_params=pltpu.CompilerParams(
            dimension_semantics=("arbitrary", "arbitrary"), vmem_limit_bytes=VMEM_LIMIT),
        name="adaln",
    )(c, ada_w, ada_b.reshape(L, 1, D6))


def _rope_kernel(pos_ref, freq_ref, tab_t_ref, tab_ref):
    ang = freq_ref[...] * pos_ref[0].astype(F32)
    row = lax.broadcasted_iota(jnp.int32, ang.shape, 0)
    cos = jnp.cos(ang)
    sin = jnp.sin(ang)
    tab_t = jnp.where(row < 64, cos, jnp.where(row < 96, -sin, sin))
    tab_t_ref[0] = tab_t
    tab_ref[0] = tab_t.T


def _rope_tables(positions, ts):
    B, S = positions.shape
    half = MLA_ROPE // 2
    inv_freq = ROPE_BASE ** (-jnp.arange(half, dtype=F32) / half)
    freq = jnp.tile(inv_freq, 4).reshape(LANES, 1)
    return pl.pallas_call(
        _rope_kernel,
        grid=(B, S // ts),
        in_specs=[
            pl.BlockSpec((1, 1, ts), lambda b, i: (b, 0, i)),
            pl.BlockSpec((LANES, 1), lambda b, i: (0, 0)),
        ],
        out_specs=(pl.BlockSpec((1, LANES, ts), lambda b, i: (b, 0, i)),
                   pl.BlockSpec((1, ts, LANES), lambda b, i: (b, i, 0))),
        out_shape=(jax.ShapeDtypeStruct((B, LANES, S), F32),
                   jax.ShapeDtypeStruct((B, S, LANES), F32)),
        compiler_params=pltpu.CompilerParams(
            dimension_semantics=("arbitrary", "arbitrary"), vmem_limit_bytes=VMEM_LIMIT),
        name="rope_tables",
    )(positions.reshape(B, 1, S), freq)


def _inproj_kernel(xm_ref, xp_ref, xn_ref, mod_ref, rope_ref, rope_t_ref, w_in_ref, mu_ref,
                   qn_ref, wuq_t_ref, kvn_ref, wuk_ref, wuv_t_ref,
                   w0_ref, w2_ref, a0_ref, a2_ref, g2_ref, kk_ref, ka_ref, rk_ref, seg_ref,
                   qt_out, k_out, vt_out, r_out, vv_out, nkk_out, wf_out, wb_out,
                   kdf_out, kdb_out, bf_out, bb_out, bonus_out, g_out, *, tm, nt):
    i = pl.program_id(1)
    D = D_MODEL
    W = RWKV_WIDTH
    te = tm + 2 * HALO
    x_ext = jnp.concatenate([xp_ref[0], xm_ref[0], xn_ref[0]], axis=0)
    mod = mod_ref[0]
    sh1 = mod[:, 0:D]
    sc1 = mod[:, D:2 * D]
    h = _rms(x_ext) * (1.0 + sc1) + sh1
    z = jnp.dot(h.astype(BF16), w_in_ref[...], preferred_element_type=F32)

    zr = z[:, MLA_COLS:]
    row = lax.broadcasted_iota(jnp.int32, (te, 1), 0)
    lo, hi = _halo_bounds(i, nt, tm)
    zr = jnp.where((row >= lo) & (row < hi), zr, 0.0)
    zc = zr[HALO:HALO + tm]
    zp = pltpu.roll(zr, 1, 0)[HALO:HALO + tm]
    zn = pltpu.roll(zr, te - 1, 0)[HALO:HALO + tm]
    mu = mu_ref[...]
    zs = zc + mu[0:1] * (zp - zc) + mu[1:2] * (zn - zc)

    zm = z[HALO:HALO + tm, 0:MLA_COLS]
    hq = (_rms(zm[:, 0:MLA_Q_RANK]) * qn_ref[...]).astype(BF16)
    q_t = lax.dot_general(wuq_t_ref[...], hq, NT_DIMS, preferred_element_type=F32)
    hkv = (_rms(zm[:, MLA_Q_RANK:MLA_Q_RANK + MLA_KV_RANK]) * kvn_ref[...]).astype(BF16)
    kn = jnp.dot(hkv, wuk_ref[...], preferred_element_type=F32)
    v_t = lax.dot_general(wuv_t_ref[...], hkv, NT_DIMS, preferred_element_type=F32)
    kprod = zm[:, 384:512] * rope_ref[0]
    kpe = (kprod + pltpu.roll(kprod, 64, 1))[:, 0:MLA_ROPE].astype(BF16)
    rope_t = rope_t_ref[0]
    qscale = (MLA_QK ** -0.5) * LOG2E
    for hd in range(MLA_HEADS):
        qh = q_t[hd * Q_HEAD_ROWS:(hd + 1) * Q_HEAD_ROWS]
        qprod = qh[MLA_NOPE:] * rope_t
        qpe = qprod[0:MLA_ROPE] + qprod[MLA_ROPE:]
        qt_out[0, hd, 0:MLA_NOPE, :] = (qh[0:MLA_NOPE] * qscale).astype(BF16)
        qt_out[0, hd, MLA_NOPE:MLA_QK, :] = (qpe * qscale).astype(BF16)
        k_out[0, hd, :, 0:MLA_NOPE] = kn[:, hd * MLA_NOPE:(hd + 1) * MLA_NOPE].astype(BF16)
        k_out[0, hd, :, MLA_NOPE:MLA_QK] = kpe
        vt_out[0, hd, 0:MLA_V, :] = v_t[hd * MLA_V:(hd + 1) * MLA_V].astype(BF16)
        vt_out[0, hd, MLA_V:V_ROWS, :] = jnp.ones((BF16_ROWS, tm), BF16)

    r = zs[:, 0:W]
    k = zs[:, W:2 * W]
    v = zs[:, 2 * W:3 * W]
    xg = zs[:, 3 * W:3 * W + GATE_LORA]
    xw = zs[:, 3 * W + GATE_LORA:3 * W + GATE_LORA + 2 * DECAY_LORA]
    xa = zs[:, 3 * W + GATE_LORA + 2 * DECAY_LORA:]
    seg = seg_ref[...]
    g_out[0] = jnp.dot(_sigmoid(xg).astype(BF16), g2_ref[...], preferred_element_type=F32)
    kk = k * kk_ref[...]
    kkn = kk * lax.rsqrt(_seg_dot(kk * kk, seg) + 1e-12)
    lw = jnp.dot(jnp.tanh(xw).astype(BF16), w2_ref[...], preferred_element_type=F32) + w0_ref[...]
    wl = jnp.minimum(lw, 0.0) - jnp.log(1.0 + jnp.exp(-jnp.abs(lw))) - 0.5
    w = jnp.exp(-jnp.exp(wl))
    a = _sigmoid(jnp.dot(xa.astype(BF16), a2_ref[...], preferred_element_type=F32) + a0_ref[...])
    ka = ka_ref[...]
    kd_f = k * (1.0 + (a[:, 0:W] - 1.0) * ka)
    kd_b = k * (1.0 + (a[:, W:] - 1.0) * ka)
    r_out[0] = r
    vv_out[0] = v
    nkk_out[0] = -kkn
    wf_out[0] = w[:, 0:W]
    wb_out[0] = w[:, W:]
    kdf_out[0] = kd_f
    kdb_out[0] = kd_b
    bf_out[0] = kkn * a[:, 0:W]
    bb_out[0] = kkn * a[:, W:]
    bonus_out[0] = _seg_dot(r * (0.5 * (kd_f + kd_b)) * rk_ref[...], seg) * v


def _inproj(x, mod_l, rope, rope_t, lw, tm):
    B, S, D = x.shape
    nt = S // tm
    hb = tm // HALO
    nhb = S // HALO
    W = RWKV_WIDTH
    H = MLA_HEADS

    def full(a):
        return pl.BlockSpec(a.shape, lambda b, i: (0,) * a.ndim)

    weights = [lw["w_in"], lw["mu"], lw["q_norm"], lw["w_uq_t"], lw["kv_norm"], lw["w_uk"],
               lw["w_uv_t"], lw["w0"], lw["w2"], lw["a0"], lw["a2"], lw["g2"], lw["k_k"],
               lw["k_a"], lw["r_k"], lw["seg"]]
    tok = pl.BlockSpec((1, tm, W), lambda b, i: (b, i, 0))
    tok_shape = jax.ShapeDtypeStruct((B, S, W), F32)
    out_shape = (
        jax.ShapeDtypeStruct((B, H, MLA_QK, S), BF16),
        jax.ShapeDtypeStruct((B, H, S, MLA_QK), BF16),
        jax.ShapeDtypeStruct((B, H, V_ROWS, S), BF16),
    ) + (tok_shape,) * 11
    out_specs = (
        pl.BlockSpec((1, H, MLA_QK, tm), lambda b, i: (b, 0, 0, i)),
        pl.BlockSpec((1, H, tm, MLA_QK), lambda b, i: (b, 0, i, 0)),
        pl.BlockSpec((1, H, V_ROWS, tm), lambda b, i: (b, 0, 0, i)),
    ) + (tok,) * 11
    return pl.pallas_call(
        functools.partial(_inproj_kernel, tm=tm, nt=nt),
        grid=(B, nt),
        in_specs=[
            pl.BlockSpec((1, tm, D), lambda b, i: (b, i, 0)),
            pl.BlockSpec((1, HALO, D), lambda b, i: (b, jnp.maximum(i * hb - 1, 0), 0)),
            pl.BlockSpec((1, HALO, D), lambda b, i: (b, jnp.minimum((i + 1) * hb, nhb - 1), 0)),
            pl.BlockSpec((1, 1, 6 * D), lambda b, i: (b, 0, 0)),
            pl.BlockSpec((1, tm, LANES), lambda b, i: (b, i, 0)),
            pl.BlockSpec((1, LANES, tm), lambda b, i: (b, 0, i)),
        ] + [full(a) for a in weights],
        out_specs=out_specs,
        out_shape=out_shape,
        compiler_params=pltpu.CompilerParams(
            dimension_semantics=("arbitrary", "arbitrary"), vmem_limit_bytes=VMEM_LIMIT),
        name="inproj",
    )(x, x, x, mod_l, rope, rope_t, *weights)


SCAN_RING = 4
_R, _W, _KD, _V, _B, _NK = range(6)


def _mixers_kernel(rf_ref, rb_ref, wf_ref, wb_ref, kdf_ref, kdb_ref, vf_ref, vb_ref, bf_ref, bb_ref,
                   nkf_ref, nkb_ref, qt_ref, k_ref, vt_ref, yf_ref, yb_ref, o_ref, *scratch,
                   tb, tq, tk, nk):
    N = RWKV_HEAD
    HP = RWKV_HEADS // 2
    B = rf_ref.shape[0]
    G = HP * B
    slabs = scratch[0:SCAN_RING]
    y_ring = scratch[SCAN_RING:2 * SCAN_RING]
    state_ref, sa_ref, s_ref, acc_ref, m_ref = scratch[2 * SCAN_RING:]
    C = sa_ref.shape[-1]
    pairs = ((rf_ref, rb_ref), (wf_ref, wb_ref), (kdf_ref, kdb_ref), (vf_ref, vb_ref),
             (bf_ref, bb_ref), (nkf_ref, nkb_ref))

    def fill(slot, j):
        for idx, (f_ref, b_ref) in enumerate(pairs):
            f_row = f_ref[:, j, :]
            b_row = b_ref[:, tb - 1 - j, :]
            rows = ([f_row[:, hp * LANES:(hp + 1) * LANES] for hp in range(HP)]
                    + [b_row[:, hp * LANES:(hp + 1) * LANES] for hp in range(HP)])
            t = jnp.concatenate(rows, axis=0).T
            slabs[slot][idx] = jnp.concatenate([t[0:N], t[N:2 * N]], axis=1)

    def scores(grp, c, slot):
        q_t = qt_ref[0, 0, :, pl.ds(pl.multiple_of(grp * tq, tq), tq)]
        start = pl.multiple_of(c * tk, tk)
        s_ref[slot] = jnp.dot(k_ref[0, 0, pl.ds(start, tk), :], q_t,
                              preferred_element_type=F32)

    def consume(c, slot):
        start = pl.multiple_of(c * tk, tk)
        s_t = s_ref[slot]
        m = m_ref[...]
        m_new = jnp.maximum(m, jnp.max(s_t, axis=0, keepdims=True))
        p_t = jnp.exp2(s_t - m_new).astype(BF16)
        alpha = jnp.exp2(m - m_new)
        acc_ref[...] = acc_ref[...] * alpha + jnp.dot(
            vt_ref[0, 0, :, pl.ds(start, tk)], p_t, preferred_element_type=F32)
        m_ref[...] = m_new

    def emit(slot, j):
        y = y_ring[slot][...]
        t = jnp.concatenate([y[:, 0:2 * G], y[:, 2 * G:C]], axis=0).T
        yf_ref[:, j, :] = jnp.concatenate([t[hp * B:(hp + 1) * B] for hp in range(HP)], axis=1)
        yb_ref[:, tb - 1 - j, :] = jnp.concatenate(
            [t[G + hp * B:G + (hp + 1) * B] for hp in range(HP)], axis=1)

    @pl.when(pl.program_id(0) == 0)
    def _():
        state_ref[...] = jnp.zeros_like(state_ref)
        for y_ref in y_ring:
            y_ref[...] = jnp.zeros_like(y_ref)

    fill(0, 0)
    fill(1, 1)
    sa0 = jnp.zeros((N, C), F32)
    for kc in range(N):
        sa0 = sa0 + state_ref[kc] * slabs[0][_NK, pl.ds(kc, 1), :]
    sa_ref[...] = sa0

    def step(grp, c, cur):
        j = grp * nk + c
        nxt = (cur + 1) % SCAN_RING
        fill((cur + 2) % SCAN_RING, jnp.minimum(j + 2, tb - 1))
        emit((cur + SCAN_RING - 1) % SCAN_RING, jnp.maximum(j - 1, 0))
        scores(grp, jnp.minimum(c + 1, nk - 1), (cur + 1) % 2)
        consume(c, cur % 2)
        for hv in range(2):
            rows = slice(hv * (N // 2), (hv + 1) * (N // 2))
            sa = sa_ref[rows, :]
            vv = slabs[cur][_V, rows, :]
            y = jnp.zeros_like(sa)
            sa_next = jnp.zeros_like(sa)
            for kc in range(N):
                row = pl.ds(kc, 1)
                s_new = (state_ref[kc, rows, :] * slabs[cur][_W, row, :]
                         + sa * slabs[cur][_B, row, :] + vv * slabs[cur][_KD, row, :])
                state_ref[kc, rows, :] = s_new
                y = y + s_new * slabs[cur][_R, row, :]
                sa_next = sa_next + s_new * slabs[nxt][_NK, row, :]
            sa_ref[rows, :] = sa_next
            y_ring[cur][rows, :] = y

    def group(grp, _):
        m_ref[...] = jnp.full(m_ref.shape, -jnp.inf, F32)
        acc_ref[...] = jnp.zeros_like(acc_ref)
        scores(grp, 0, 0)

        def body(i, _):
            for u in range(SCAN_RING):
                pl.when(i >= -u)(functools.partial(step, grp, i * SCAN_RING + u, u))
            return 0

        lax.fori_loop(0, nk // SCAN_RING, body, 0)
        acc = acc_ref[...]
        o_ref[0, pl.ds(pl.multiple_of(grp * tq, tq), tq), :] = (
            acc[0:MLA_V] / acc[MLA_V:MLA_V + 1]).T
        return 0

    lax.fori_loop(0, tb // nk, group, 0)
    emit((tb - 1) % SCAN_RING, tb - 1)


def _mixers(r, w_f, w_b, kd_f, kd_b, v, b_f, b_b, nkk, q_t, k, v_t, tb, tk):
    B, S, W = r.shape
    H = MLA_HEADS
    nb = S // tb
    nk = S // tk
    gps = tb // nk
    tq = B * H * tb // gps
    qblk = S // (gps * tq)
    assert tb % nk == 0 and nk % SCAN_RING == 0 and S % (gps * tq) == 0 and nb == B * H * qblk
    chains = 2 * RWKV_HEADS * B
    fwd = pl.BlockSpec((B, tb, W), lambda i: (0, i, 0))
    bwd = pl.BlockSpec((B, tb, W), lambda i: (0, nb - 1 - i, 0))

    def bh(i):
        return i // (H * qblk), (i // qblk) % H

    return pl.pallas_call(
        functools.partial(_mixers_kernel, tb=tb, tq=tq, tk=tk, nk=nk),
        grid=(nb,),
        in_specs=[fwd, bwd] * 6 + [
            pl.BlockSpec((1, 1, MLA_QK, gps * tq), lambda i: (*bh(i), 0, i % qblk)),
            pl.BlockSpec((1, 1, S, MLA_QK), lambda i: (*bh(i), 0, 0)),
            pl.BlockSpec((1, 1, V_ROWS, S), lambda i: (*bh(i), 0, 0)),
        ],
        out_specs=(fwd, bwd,
                   pl.BlockSpec((1, gps * tq, MLA_V), lambda i: (bh(i)[0], i % qblk, bh(i)[1]))),
        out_shape=(jax.ShapeDtypeStruct((B, S, W), F32),) * 2
        + (jax.ShapeDtypeStruct((B, S, H * MLA_V), F32),),
        scratch_shapes=[pltpu.VMEM((6, RWKV_HEAD, chains), F32)] * SCAN_RING
        + [pltpu.VMEM((RWKV_HEAD, chains), F32)] * SCAN_RING
        + [pltpu.VMEM((RWKV_HEAD, RWKV_HEAD, chains), F32),
           pltpu.VMEM((RWKV_HEAD, chains), F32),
           pltpu.VMEM((2, tk, tq), F32), pltpu.VMEM((V_ROWS, tq), F32), pltpu.VMEM((1, tq), F32)],
        compiler_params=pltpu.CompilerParams(
            dimension_semantics=("arbitrary",), vmem_limit_bytes=VMEM_LIMIT),
        name="token_mixers",
    )(r, r, w_f, w_b, kd_f, kd_b, v, v, b_f, b_b, nkk, nkk, q_t, k, v_t)


def _mixout_kernel(x_ref, o_ref, yf_ref, yb_ref, bonus_ref, g_ref, mod_ref, on_ref, gng_ref,
                   gnb_ref, seg_ref, wo_ref, out_ref):
    D = D_MODEL
    gt1 = mod_ref[0][:, 2 * D:3 * D]
    seg = seg_ref[...]
    y_mla = _rms(o_ref[0]) * on_ref[...]
    y = yf_ref[0] + yb_ref[0]
    inv_n = 1.0 / RWKV_HEAD
    d = y - _seg_dot(y, seg) * inv_n
    var = _seg_dot(d * d, seg) * inv_n
    yn = d * lax.rsqrt(var + GN_EPS) * gng_ref[...] + gnb_ref[...]
    y_rwkv = (yn + bonus_ref[0]) * g_ref[0]
    proj = (jnp.dot(y_mla.astype(BF16), wo_ref[0:MLA_WIDTH, :], preferred_element_type=F32)
            + jnp.dot(y_rwkv.astype(BF16), wo_ref[MLA_WIDTH:, :], preferred_element_type=F32))
    out_ref[0] = x_ref[0] + gt1 * proj


def _mixout(x, o, y_f, y_b, bonus, g, mod_l, lw, tm):
    B, S, D = x.shape
    W = RWKV_WIDTH

    def full(a):
        return pl.BlockSpec(a.shape, lambda b, i: (0,) * a.ndim)

    weights = [lw["out_norm"], lw["gn_g"], lw["gn_b"], lw["seg"], lw["w_out"]]
    tokw = pl.BlockSpec((1, tm, W), lambda b, i: (b, i, 0))
    tokd = pl.BlockSpec((1, tm, D), lambda b, i: (b, i, 0))
    return pl.pallas_call(
        _mixout_kernel,
        grid=(B, S // tm),
        in_specs=[tokd, tokw, tokw, tokw, tokw, tokw,
                  pl.BlockSpec((1, 1, 6 * D), lambda b, i: (b, 0, 0))] + [full(a) for a in weights],
        out_specs=tokd,
        out_shape=jax.ShapeDtypeStruct((B, S, D), F32),
        compiler_params=pltpu.CompilerParams(
            dimension_semantics=("arbitrary", "arbitrary"), vmem_limit_bytes=VMEM_LIMIT),
        name="mixer_out",
    )(x, o, y_f, y_b, bonus, g, mod_l, *weights)


def _ffn_kernel(xm_ref, xp_ref, xn_ref, mod_ref, wg_ref, wv_ref, cw_ref, cb_ref, wd_ref,
                out_ref, h_ref, acc_ref, *, tm, nt, nc):
    i = pl.program_id(1)
    c = pl.program_id(2)
    D = D_MODEL
    te = tm + 2 * HALO
    mod = mod_ref[0]

    @pl.when(c == 0)
    def _():
        x_ext = jnp.concatenate([xp_ref[0], xm_ref[0], xn_ref[0]], axis=0)
        sh2 = mod[:, 3 * D:4 * D]
        sc2 = mod[:, 4 * D:5 * D]
        h_ref[...] = (_rms(x_ext) * (1.0 + sc2) + sh2).astype(BF16)
        acc_ref[...] = jnp.zeros_like(acc_ref)

    h = h_ref[...]
    gate = jnp.dot(h, wg_ref[...], preferred_element_type=F32)
    val = jnp.dot(h[HALO:HALO + tm], wv_ref[...], preferred_element_type=F32)
    row = lax.broadcasted_iota(jnp.int32, (te, 1), 0)
    lo, hi = _halo_bounds(i, nt, tm)
    gate = jnp.where((row >= lo) & (row < hi), gate, 0.0)
    gp = pltpu.roll(gate, 1, 0)[HALO:HALO + tm]
    gn = pltpu.roll(gate, te - 1, 0)[HALO:HALO + tm]
    gc = gate[HALO:HALO + tm]
    cw = cw_ref[...]
    gg = cw[0:1] * gp + cw[1:2] * gc + cw[2:3] * gn + cb_ref[...]
    act = gg * _sigmoid(gg) * val
    acc_ref[...] += jnp.dot(act.astype(BF16), wd_ref[...], preferred_element_type=F32)

    @pl.when(c == nc - 1)
    def _():
        gt2 = mod[:, 5 * D:6 * D]
        out_ref[0] = xm_ref[0] + gt2 * acc_ref[...]


def _ffn(x, mod_l, lw, tm, fc):
    B, S, D = x.shape
    nt = S // tm
    nc = D_FF // fc
    hb = tm // HALO
    nhb = S // HALO
    return pl.pallas_call(
        functools.partial(_ffn_kernel, tm=tm, nt=nt, nc=nc),
        grid=(B, nt, nc),
        in_specs=[
            pl.BlockSpec((1, tm, D), lambda b, i, c: (b, i, 0)),
            pl.BlockSpec((1, HALO, D), lambda b, i, c: (b, jnp.maximum(i * hb - 1, 0), 0)),
            pl.BlockSpec((1, HALO, D), lambda b, i, c: (b, jnp.minimum((i + 1) * hb, nhb - 1), 0)),
            pl.BlockSpec((1, 1, 6 * D), lambda b, i, c: (b, 0, 0)),
            pl.BlockSpec((D, fc), lambda b, i, c: (0, c)),
            pl.BlockSpec((D, fc), lambda b, i, c: (0, nc + c)),
            pl.BlockSpec((3, fc), lambda b, i, c: (0, c)),
            pl.BlockSpec((1, fc), lambda b, i, c: (0, c)),
            pl.BlockSpec((fc, D), lambda b, i, c: (c, 0)),
        ],
        out_specs=pl.BlockSpec((1, tm, D), lambda b, i, c: (b, i, 0)),
        out_shape=jax.ShapeDtypeStruct((B, S, D), F32),
        scratch_shapes=[pltpu.VMEM((tm + 2 * HALO, D), BF16), pltpu.VMEM((tm, D), F32)],
        compiler_params=pltpu.CompilerParams(
            dimension_semantics=("arbitrary", "arbitrary", "arbitrary"),
            vmem_limit_bytes=VMEM_LIMIT),
        name="conv_ffn",
    )(x, x, x, mod_l, lw["w_up"], lw["w_up"], lw["conv_w"], lw["conv_b"], lw["w_down"])


def _final_norm_kernel(x_ref, g_ref, o_ref):
    o_ref[0] = _rms(x_ref[0]) * g_ref[...]


def _final_norm(x, gain, tm):
    B, S, D = x.shape
    tok = pl.BlockSpec((1, tm, D), lambda b, i: (b, i, 0))
    return pl.pallas_call(
        _final_norm_kernel,
        grid=(B, S // tm),
        in_specs=[tok, pl.BlockSpec((1, D), lambda b, i: (0, 0))],
        out_specs=tok,
        out_shape=jax.ShapeDtypeStruct((B, S, D), F32),
        compiler_params=pltpu.CompilerParams(
            dimension_semantics=("arbitrary", "arbitrary"), vmem_limit_bytes=VMEM_LIMIT),
        name="final_norm",
    )(x, gain.reshape(1, D))


def _block_diag2(a, b):
    za = jnp.zeros((a.shape[0], b.shape[1]), a.dtype)
    zb = jnp.zeros((b.shape[0], a.shape[1]), a.dtype)
    return jnp.concatenate([jnp.concatenate([a, za], axis=1),
                            jnp.concatenate([zb, b], axis=1)], axis=0)


def _swap_halves(cols):
    half = cols.shape[-1] // 2
    return jnp.concatenate([cols[..., half:], cols[..., :half]], axis=-1)


def _layer_weights(l, p):
    W = RWKV_WIDTH
    w_in = p["w_in"][l]
    mla = w_in[:, :448]
    rw = w_in[:, 448:]
    kpe = mla[:, 384:448]
    rw_perm = jnp.concatenate([rw[:, :3 * W], rw[:, 3 * W + 256:], rw[:, 3 * W:3 * W + 256]], axis=1)
    mu = p["rwkv_mu"][l]
    mu_perm = jnp.concatenate([mu[:, :3 * W], mu[:, 3 * W + 256:], mu[:, 3 * W:3 * W + 256]], axis=1)
    w_in_perm = jnp.concatenate([mla, _swap_halves(kpe), rw_perm], axis=1).astype(BF16)

    w_uq = p["mla_w_uq"][l].reshape(MLA_Q_RANK, MLA_HEADS, MLA_QK)
    pe = w_uq[:, :, MLA_NOPE:]
    w_uq_t = jnp.concatenate([w_uq, _swap_halves(pe)], axis=-1).reshape(
        MLA_Q_RANK, MLA_HEADS * Q_HEAD_ROWS).T.astype(BF16)
    w_ukv = p["mla_w_ukv"][l].reshape(MLA_KV_RANK, MLA_HEADS, MLA_NOPE + MLA_V)
    w_uk = w_ukv[:, :, :MLA_NOPE].reshape(MLA_KV_RANK, MLA_HEADS * MLA_NOPE).astype(BF16)
    w_uv_t = w_ukv[:, :, MLA_NOPE:].reshape(MLA_KV_RANK, MLA_HEADS * MLA_V).T.astype(BF16)

    head = jnp.arange(W) // RWKV_HEAD
    seg = (head[:, None] == head[None, :]).astype(BF16)
    return {
        "w_in": w_in_perm,
        "mu": mu_perm,
        "q_norm": p["mla_q_norm"][l].reshape(1, -1),
        "w_uq_t": w_uq_t,
        "kv_norm": p["mla_kv_norm"][l].reshape(1, -1),
        "w_uk": w_uk,
        "w_uv_t": w_uv_t,
        "out_norm": p["mla_out_norm"][l].reshape(1, -1),
        "w0": p["rwkv_w0"][l].reshape(1, 2 * W),
        "w2": _block_diag2(p["rwkv_w2"][l, 0], p["rwkv_w2"][l, 1]).astype(BF16),
        "a0": p["rwkv_a0"][l].reshape(1, 2 * W),
        "a2": _block_diag2(p["rwkv_a2"][l, 0], p["rwkv_a2"][l, 1]).astype(BF16),
        "g2": p["rwkv_g2"][l].astype(BF16),
        "k_k": p["rwkv_k_k"][l].reshape(1, W),
        "k_a": p["rwkv_k_a"][l].reshape(1, W),
        "r_k": p["rwkv_r_k"][l].reshape(1, W),
        "gn_g": p["rwkv_gn_g"][l].reshape(1, W),
        "gn_b": p["rwkv_gn_b"][l].reshape(1, W),
        "seg": seg,
        "w_out": p["w_out"][l].astype(BF16),
        "w_up": p["ffn_w_up"][l].astype(BF16),
        "conv_w": p["ffn_conv_w"][l],
        "conv_b": p["ffn_conv_b"][l].reshape(1, D_FF),
        "w_down": p["ffn_w_down"][l].astype(BF16),
    }


def _forward(x, c, positions, p, cfg):
    B, S, D = x.shape
    mod = _adaln(c, p["ada_w"], p["ada_b"])
    rope_t, rope = _rope_tables(positions, cfg["tm_in"])
    for l in range(DEPTH):
        lw = _layer_weights(l, p)
        mod_l = mod[l].reshape(B, 1, 6 * D)
        (q_t, k, v_t, r, vv, nkk, w_f, w_b, kd_f, kd_b, b_f, b_b, bonus, g) = _inproj(
            x, mod_l, rope, rope_t, lw, cfg["tm_in"])
        y_f, y_b, o = _mixers(r, w_f, w_b, kd_f, kd_b, vv, b_f, b_b, nkk, q_t, k, v_t,
                              cfg["tb"], cfg["tk"])
        x = _mixout(x, o, y_f, y_b, bonus, g, mod_l, lw, cfg["tm_out"])
        x = _ffn(x, mod_l, lw, cfg["tm_ffn"], cfg["fc"])
    return _final_norm(x, p["final_norm"], cfg["tm_out"])


_CFG = {"tm_in": 256, "tk": 512, "tb": 32, "tm_out": 512, "tm_ffn": 512, "fc": 1408}


def kernel(x, c, positions, ada_w, ada_b, w_in, mla_q_norm, mla_w_uq, mla_kv_norm, mla_w_ukv, mla_out_norm, rwkv_mu, rwkv_w0, rwkv_w2, rwkv_a0, rwkv_a2, rwkv_g2, rwkv_k_k, rwkv_k_a, rwkv_r_k, rwkv_gn_g, rwkv_gn_b, w_out, ffn_w_up, ffn_conv_w, ffn_conv_b, ffn_w_down, final_norm):
    p = dict(ada_w=ada_w, ada_b=ada_b, w_in=w_in, mla_q_norm=mla_q_norm, mla_w_uq=mla_w_uq,
             mla_kv_norm=mla_kv_norm, mla_w_ukv=mla_w_ukv, mla_out_norm=mla_out_norm,
             rwkv_mu=rwkv_mu, rwkv_w0=rwkv_w0, rwkv_w2=rwkv_w2, rwkv_a0=rwkv_a0, rwkv_a2=rwkv_a2,
             rwkv_g2=rwkv_g2, rwkv_k_k=rwkv_k_k, rwkv_k_a=rwkv_k_a, rwkv_r_k=rwkv_r_k,
             rwkv_gn_g=rwkv_gn_g, rwkv_gn_b=rwkv_gn_b, w_out=w_out, ffn_w_up=ffn_w_up,
             ffn_conv_w=ffn_conv_w, ffn_conv_b=ffn_conv_b, ffn_w_down=ffn_w_down,
             final_norm=final_norm)
    return _forward(x, c, positions, p, _CFG)
```

```python
import functools

import jax
import jax.numpy as jnp
from jax import lax
from jax.experimental import pallas as pl
from jax.experimental.pallas import tpu as pltpu

D_MODEL = 1024
DEPTH = 4
MLA_HEADS = 4
MLA_NOPE = 128
MLA_ROPE = 64
MLA_V = 128
MLA_Q_RANK = 256
MLA_KV_RANK = 128
MLA_WIDTH = MLA_HEADS * MLA_V
MLA_QK = MLA_NOPE + MLA_ROPE
ROPE_BASE = 10000.0
RWKV_WIDTH = D_MODEL - MLA_WIDTH
RWKV_HEAD = 64
RWKV_HEADS = RWKV_WIDTH // RWKV_HEAD
DECAY_LORA = 64
AAA_LORA = 64
GATE_LORA = 128
GN_EPS = 64e-5
D_FF = 2816
EPS = 1e-6

MLA_COLS = 512
RWKV_COLS = 3 * RWKV_WIDTH + GATE_LORA + 2 * DECAY_LORA + 2 * AAA_LORA
IN_COLS = MLA_COLS + RWKV_COLS
Q_HEAD_ROWS = MLA_NOPE + 2 * MLA_ROPE
BF16_ROWS = 16
V_ROWS = MLA_V + BF16_ROWS

LANES = 128
HALO = 8
VMEM_LIMIT = 56 * 1024 * 1024
LOG2E = 1.4426950408889634

F32 = jnp.float32
BF16 = jnp.bfloat16
NT_DIMS = (((1,), (1,)), ((), ()))


def _sigmoid(x):
    return 1.0 / (1.0 + jnp.exp(-x))


def _rms(x, eps=EPS):
    return x * lax.rsqrt(jnp.mean(x * x, axis=-1, keepdims=True) + eps)


def _seg_dot(x, seg):
    hi = x.astype(BF16)
    lo = (x - hi.astype(F32)).astype(BF16)
    return (jnp.dot(hi, seg, preferred_element_type=F32)
            + jnp.dot(lo, seg, preferred_element_type=F32))


def _halo_bounds(i, nt, tm):
    lo = jnp.where(i > 0, 0, HALO)
    hi = jnp.where(i < nt - 1, tm + 2 * HALO, tm + HALO)
    return lo, hi


def _adaln_kernel(c_ref, w_ref, b_ref, o_ref):
    c = c_ref[...]
    ca = c * _sigmoid(c)
    o_ref[0] = jnp.dot(ca, w_ref[0], precision=lax.Precision.HIGHEST,
                       preferred_element_type=F32) + b_ref[0]


def _adaln(c, ada_w, ada_b):
    L, D, D6 = ada_w.shape
    B = c.shape[0]
    nj = D6 // D
    return pl.pallas_call(
        _adaln_kernel,
        grid=(L, nj),
        in_specs=[
            pl.BlockSpec((B, D), lambda l, j: (0, 0)),
            pl.BlockSpec((1, D, D), lambda l, j: (l, 0, j)),
            pl.BlockSpec((1, 1, D), lambda l, j: (l, 0, j)),
        ],
        out_specs=pl.BlockSpec((1, B, D), lambda l, j: (l, 0, j)),
        out_shape=jax.ShapeDtypeStruct((L, B, D6), F32),
        compiler_params=pltpu.CompilerParams(
            dimension_semantics=("arbitrary", "arbitrary"), vmem_limit_bytes=VMEM_LIMIT),
        name="adaln",
    )(c, ada_w, ada_b.reshape(L, 1, D6))


def _rope_kernel(pos_ref, freq_ref, tab_t_ref, tab_ref):
    ang = freq_ref[...] * pos_ref[0].astype(F32)
    row = lax.broadcasted_iota(jnp.int32, ang.shape, 0)
    cos = jnp.cos(ang)
    sin = jnp.sin(ang)
    tab_t = jnp.where(row < 64, cos, jnp.where(row < 96, -sin, sin))
    tab_t_ref[0] = tab_t
    tab_ref[0] = tab_t.T


def _rope_tables(positions, ts):
    B, S = positions.shape
    half = MLA_ROPE // 2
    inv_freq = ROPE_BASE ** (-jnp.arange(half, dtype=F32) / half)
    freq = jnp.tile(inv_freq, 4).reshape(LANES, 1)
    return pl.pallas_call(
        _rope_kernel,
        grid=(B, S // ts),
        in_specs=[
            pl.BlockSpec((1, 1, ts), lambda b, i: (b, 0, i)),
            pl.BlockSpec((LANES, 1), lambda b, i: (0, 0)),
        ],
        out_specs=(pl.BlockSpec((1, LANES, ts), lambda b, i: (b, 0, i)),
                   pl.BlockSpec((1, ts, LANES), lambda b, i: (b, i, 0))),
        out_shape=(jax.ShapeDtypeStruct((B, LANES, S), F32),
                   jax.ShapeDtypeStruct((B, S, LANES), F32)),
        compiler_params=pltpu.CompilerParams(
            dimension_semantics=("arbitrary", "arbitrary"), vmem_limit_bytes=VMEM_LIMIT),
        name="rope_tables",
    )(positions.reshape(B, 1, S), freq)


def _inproj_kernel(xm_ref, xp_ref, xn_ref, mod_ref, rope_ref, rope_t_ref, w_in_ref, mu_ref,
                   qn_ref, wuq_t_ref, kvn_ref, wuk_ref, wuv_t_ref,
                   w0_ref, w2_ref, a0_ref, a2_ref, g2_ref, kk_ref, ka_ref, rk_ref, seg_ref,
                   qt_out, k_out, vt_out, r_out, vv_out, nkk_out, wf_out, wb_out,
                   kdf_out, kdb_out, bf_out, bb_out, bonus_out, g_out, *, tm, nt):
    i = pl.program_id(1)
    D = D_MODEL
    W = RWKV_WIDTH
    te = tm + 2 * HALO
    x_ext = jnp.concatenate([xp_ref[0], xm_ref[0], xn_ref[0]], axis=0)
    mod = mod_ref[0]
    sh1 = mod[:, 0:D]
    sc1 = mod[:, D:2 * D]
    h = _rms(x_ext) * (1.0 + sc1) + sh1
    z = jnp.dot(h.astype(BF16), w_in_ref[...], preferred_element_type=F32)

    zr = z[:, MLA_COLS:]
    row = lax.broadcasted_iota(jnp.int32, (te, 1), 0)
    lo, hi = _halo_bounds(i, nt, tm)
    zr = jnp.where((row >= lo) & (row < hi), zr, 0.0)
    zc = zr[HALO:HALO + tm]
    zp = pltpu.roll(zr, 1, 0)[HALO:HALO + tm]
    zn = pltpu.roll(zr, te - 1, 0)[HALO:HALO + tm]
    mu = mu_ref[...]
    zs = zc + mu[0:1] * (zp - zc) + mu[1:2] * (zn - zc)

    zm = z[HALO:HALO + tm, 0:MLA_COLS]
    hq = (_rms(zm[:, 0:MLA_Q_RANK]) * qn_ref[...]).astype(BF16)
    q_t = lax.dot_general(wuq_t_ref[...], hq, NT_DIMS, preferred_element_type=F32)
    hkv = (_rms(zm[:, MLA_Q_RANK:MLA_Q_RANK + MLA_KV_RANK]) * kvn_ref[...]).astype(BF16)
    kn = jnp.dot(hkv, wuk_ref[...], preferred_element_type=F32)
    v_t = lax.dot_general(wuv_t_ref[...], hkv, NT_DIMS, preferred_element_type=F32)
    kprod = zm[:, 384:512] * rope_ref[0]
    kpe = (kprod + pltpu.roll(kprod, 64, 1))[:, 0:MLA_ROPE].astype(BF16)
    rope_t = rope_t_ref[0]
    qscale = (MLA_QK ** -0.5) * LOG2E
    for hd in range(MLA_HEADS):
        qh = q_t[hd * Q_HEAD_ROWS:(hd + 1) * Q_HEAD_ROWS]
        qprod = qh[MLA_NOPE:] * rope_t
        qpe = qprod[0:MLA_ROPE] + qprod[MLA_ROPE:]
        qt_out[0, hd, 0:MLA_NOPE, :] = (qh[0:MLA_NOPE] * qscale).astype(BF16)
        qt_out[0, hd, MLA_NOPE:MLA_QK, :] = (qpe * qscale).astype(BF16)
        k_out[0, hd, :, 0:MLA_NOPE] = kn[:, hd * MLA_NOPE:(hd + 1) * MLA_NOPE].astype(BF16)
        k_out[0, hd, :, MLA_NOPE:MLA_QK] = kpe
        vt_out[0, hd, 0:MLA_V, :] = v_t[hd * MLA_V:(hd + 1) * MLA_V].astype(BF16)
        vt_out[0, hd, MLA_V:V_ROWS, :] = jnp.ones((BF16_ROWS, tm), BF16)

    r = zs[:, 0:W]
    k = zs[:, W:2 * W]
    v = zs[:, 2 * W:3 * W]
    xg = zs[:, 3 * W:3 * W + GATE_LORA]
    xw = zs[:, 3 * W + GATE_LORA:3 * W + GATE_LORA + 2 * DECAY_LORA]
    xa = zs[:, 3 * W + GATE_LORA + 2 * DECAY_LORA:]
    seg = seg_ref[...]
    g_out[0] = jnp.dot(_sigmoid(xg).astype(BF16), g2_ref[...], preferred_element_type=F32)
    kk = k * kk_ref[...]
    kkn = kk * lax.rsqrt(_seg_dot(kk * kk, seg) + 1e-12)
    lw = jnp.dot(jnp.tanh(xw).astype(BF16), w2_ref[...], preferred_element_type=F32) + w0_ref[...]
    wl = jnp.minimum(lw, 0.0) - jnp.log(1.0 + jnp.exp(-jnp.abs(lw))) - 0.5
    w = jnp.exp(-jnp.exp(wl))
    a = _sigmoid(jnp.dot(xa.astype(BF16), a2_ref[...], preferred_element_type=F32) + a0_ref[...])
    ka = ka_ref[...]
    kd_f = k * (1.0 + (a[:, 0:W] - 1.0) * ka)
    kd_b = k * (1.0 + (a[:, W:] - 1.0) * ka)
    r_out[0] = r
    vv_out[0] = v
    nkk_out[0] = -kkn
    wf_out[0] = w[:, 0:W]
    wb_out[0] = w[:, W:]
    kdf_out[0] = kd_f
    kdb_out[0] = kd_b
    bf_out[0] = kkn * a[:, 0:W]
    bb_out[0] = kkn * a[:, W:]
    bonus_out[0] = _seg_dot(r * (0.5 * (kd_f + kd_b)) * rk_ref[...], seg) * v


def _inproj(x, mod_l, rope, rope_t, lw, tm):
    B, S, D = x.shape
    nt = S // tm
    hb = tm // HALO
    nhb = S // HALO
    W = RWKV_WIDTH
    H = MLA_HEADS

    def full(a):
        return pl.BlockSpec(a.shape, lambda b, i: (0,) * a.ndim)

    weights = [lw["w_in"], lw["mu"], lw["q_norm"], lw["w_uq_t"], lw["kv_norm"], lw["w_uk"],
               lw["w_uv_t"], lw["w0"], lw["w2"], lw["a0"], lw["a2"], lw["g2"], lw["k_k"],
               lw["k_a"], lw["r_k"], lw["seg"]]
    tok = pl.BlockSpec((1, tm, W), lambda b, i: (b, i, 0))
    tok_shape = jax.ShapeDtypeStruct((B, S, W), F32)
    out_shape = (
        jax.ShapeDtypeStruct((B, H, MLA_QK, S), BF16),
        jax.ShapeDtypeStruct((B, H, S, MLA_QK), BF16),
        jax.ShapeDtypeStruct((B, H, V_ROWS, S), BF16),
    ) + (tok_shape,) * 11
    out_specs = (
        pl.BlockSpec((1, H, MLA_QK, tm), lambda b, i: (b, 0, 0, i)),
        pl.BlockSpec((1, H, tm, MLA_QK), lambda b, i: (b, 0, i, 0)),
        pl.BlockSpec((1, H, V_ROWS, tm), lambda b, i: (b, 0, 0, i)),
    ) + (tok,) * 11
    return pl.pallas_call(
        functools.partial(_inproj_kernel, tm=tm, nt=nt),
        grid=(B, nt),
        in_specs=[
            pl.BlockSpec((1, tm, D), lambda b, i: (b, i, 0)),
            pl.BlockSpec((1, HALO, D), lambda b, i: (b, jnp.maximum(i * hb - 1, 0), 0)),
            pl.BlockSpec((1, HALO, D), lambda b, i: (b, jnp.minimum((i + 1) * hb, nhb - 1), 0)),
            pl.BlockSpec((1, 1, 6 * D), lambda b, i: (b, 0, 0)),
            pl.BlockSpec((1, tm, LANES), lambda b, i: (b, i, 0)),
            pl.BlockSpec((1, LANES, tm), lambda b, i: (b, 0, i)),
        ] + [full(a) for a in weights],
        out_specs=out_specs,
        out_shape=out_shape,
        compiler_params=pltpu.CompilerParams(
            dimension_semantics=("arbitrary", "arbitrary"), vmem_limit_bytes=VMEM_LIMIT),
        name="inproj",
    )(x, x, x, mod_l, rope, rope_t, *weights)


SCAN_RING = 4
_R, _W, _KD, _V, _B, _NK = range(6)


def _mixers_kernel(rf_ref, rb_ref, wf_ref, wb_ref, kdf_ref, kdb_ref, vf_ref, vb_ref, bf_ref, bb_ref,
                   nkf_ref, nkb_ref, qt_ref, k_ref, vt_ref, yf_ref, yb_ref, o_ref, *scratch,
                   tb, tq, tk, nk):
    N = RWKV_HEAD
    HP = RWKV_HEADS // 2
    B = rf_ref.shape[0]
    G = HP * B
    slabs = scratch[0:SCAN_RING]
    y_ring = scratch[SCAN_RING:2 * SCAN_RING]
    state_ref, sa_ref, s_ref, cmax_ref, acc_ref, m_ref = scratch[2 * SCAN_RING:]
    C = sa_ref.shape[-1]
    pairs = ((rf_ref, rb_ref), (wf_ref, wb_ref), (kdf_ref, kdb_ref), (vf_ref, vb_ref),
             (bf_ref, bb_ref), (nkf_ref, nkb_ref))

    def fill(slot, j):
        for idx, (f_ref, b_ref) in enumerate(pairs):
            f_row = f_ref[:, j, :]
            b_row = b_ref[:, tb - 1 - j, :]
            rows = ([f_row[:, hp * LANES:(hp + 1) * LANES] for hp in range(HP)]
                    + [b_row[:, hp * LANES:(hp + 1) * LANES] for hp in range(HP)])
            t = jnp.concatenate(rows, axis=0).T
            slabs[slot][idx] = jnp.concatenate([t[0:N], t[N:2 * N]], axis=1)

    def scores(grp, c, slot):
        q_t = qt_ref[0, 0, :, pl.ds(pl.multiple_of(grp * tq, tq), tq)]
        start = pl.multiple_of(c * tk, tk)
        s_t = jnp.dot(k_ref[0, 0, pl.ds(start, tk), :], q_t,
                      preferred_element_type=F32)
        s_ref[slot] = s_t
        cmax_ref[slot] = jnp.max(s_t, axis=0, keepdims=True)

    def consume(c, slot):
        start = pl.multiple_of(c * tk, tk)
        s_t = s_ref[slot]
        m = m_ref[...]
        m_new = jnp.maximum(m, cmax_ref[slot])
        p_t = jnp.exp2(s_t - m_new).astype(BF16)
        alpha = jnp.exp2(m - m_new)
        acc_ref[...] = acc_ref[...] * alpha + jnp.dot(
            vt_ref[0, 0, :, pl.ds(start, tk)], p_t, preferred_element_type=F32)
        m_ref[...] = m_new

    def emit(slot, j):
        y = y_ring[slot][...]
        t = jnp.concatenate([y[:, 0:2 * G], y[:, 2 * G:C]], axis=0).T
        yf_ref[:, j, :] = jnp.concatenate([t[hp * B:(hp + 1) * B] for hp in range(HP)], axis=1)
        yb_ref[:, tb - 1 - j, :] = jnp.concatenate(
            [t[G + hp * B:G + (hp + 1) * B] for hp in range(HP)], axis=1)

    @pl.when(pl.program_id(0) == 0)
    def _():
        state_ref[...] = jnp.zeros_like(state_ref)
        for y_ref in y_ring:
            y_ref[...] = jnp.zeros_like(y_ref)

    fill(0, 0)
    fill(1, 1)
    sa0 = jnp.zeros((N, C), F32)
    for kc in range(N):
        sa0 = sa0 + state_ref[kc] * slabs[0][_NK, pl.ds(kc, 1), :]
    sa_ref[...] = sa0

    def step(grp, c, cur):
        j = grp * nk + c
        nxt = (cur + 1) % SCAN_RING
        fill((cur + 2) % SCAN_RING, jnp.minimum(j + 2, tb - 1))
        emit((cur + SCAN_RING - 1) % SCAN_RING, jnp.maximum(j - 1, 0))
        scores(grp, jnp.minimum(c + 1, nk - 1), (cur + 1) % 2)
        consume(c, cur % 2)
        for hv in range(2):
            rows = slice(hv * (N // 2), (hv + 1) * (N // 2))
            sa = sa_ref[rows, :]
            vv = slabs[cur][_V, rows, :]
            y = jnp.zeros_like(sa)
            sa_next = jnp.zeros_like(sa)
            for kc in range(N):
                row = pl.ds(kc, 1)
                s_new = (state_ref[kc, rows, :] * slabs[cur][_W, row, :]
                         + sa * slabs[cur][_B, row, :] + vv * slabs[cur][_KD, row, :])
                state_ref[kc, rows, :] = s_new
                y = y + s_new * slabs[cur][_R, row, :]
                sa_next = sa_next + s_new * slabs[nxt][_NK, row, :]
            sa_ref[rows, :] = sa_next
            y_ring[cur][rows, :] = y

    def group(grp, _):
        m_ref[...] = jnp.full(m_ref.shape, -jnp.inf, F32)
        acc_ref[...] = jnp.zeros_like(acc_ref)
        scores(grp, 0, 0)

        def body(i, _):
            for u in range(SCAN_RING):
                pl.when(i >= -u)(functools.partial(step, grp, i * SCAN_RING + u, u))
            return 0

        lax.fori_loop(0, nk // SCAN_RING, body, 0)
        acc = acc_ref[...]
        o_ref[0, pl.ds(pl.multiple_of(grp * tq, tq), tq), :] = (
            acc[0:MLA_V] / acc[MLA_V:MLA_V + 1]).T
        return 0

    lax.fori_loop(0, tb // nk, group, 0)
    emit((tb - 1) % SCAN_RING, tb - 1)


def _mixers(r, w_f, w_b, kd_f, kd_b, v, b_f, b_b, nkk, q_t, k, v_t, tb, tk):
    B, S, W = r.shape
    H = MLA_HEADS
    nb = S // tb
    nk = S // tk
    gps = tb // nk
    tq = B * H * tb // gps
    qblk = S // (gps * tq)
    assert tb % nk == 0 and nk % SCAN_RING == 0 and S % (gps * tq) == 0 and nb == B * H * qblk
    chains = 2 * RWKV_HEADS * B
    fwd = pl.BlockSpec((B, tb, W), lambda i: (0, i, 0))
    bwd = pl.BlockSpec((B, tb, W), lambda i: (0, nb - 1 - i, 0))

    def bh(i):
        return i // (H * qblk), (i // qblk) % H

    return pl.pallas_call(
        functools.partial(_mixers_kernel, tb=tb, tq=tq, tk=tk, nk=nk),
        grid=(nb,),
        in_specs=[fwd, bwd] * 6 + [
            pl.BlockSpec((1, 1, MLA_QK, gps * tq), lambda i: (*bh(i), 0, i % qblk)),
            pl.BlockSpec((1, 1, S, MLA_QK), lambda i: (*bh(i), 0, 0)),
            pl.BlockSpec((1, 1, V_ROWS, S), lambda i: (*bh(i), 0, 0)),
        ],
        out_specs=(fwd, bwd,
                   pl.BlockSpec((1, gps * tq, MLA_V), lambda i: (bh(i)[0], i % qblk, bh(i)[1]))),
        out_shape=(jax.ShapeDtypeStruct((B, S, W), F32),) * 2
        + (jax.ShapeDtypeStruct((B, S, H * MLA_V), F32),),
        scratch_shapes=[pltpu.VMEM((6, RWKV_HEAD, chains), F32)] * SCAN_RING
        + [pltpu.VMEM((RWKV_HEAD, chains), F32)] * SCAN_RING
        + [pltpu.VMEM((RWKV_HEAD, RWKV_HEAD, chains), F32),
           pltpu.VMEM((RWKV_HEAD, chains), F32),
           pltpu.VMEM((2, tk, tq), F32), pltpu.VMEM((2, 1, tq), F32),
           pltpu.VMEM((V_ROWS, tq), F32), pltpu.VMEM((1, tq), F32)],
        compiler_params=pltpu.CompilerParams(
            dimension_semantics=("arbitrary",), vmem_limit_bytes=VMEM_LIMIT),
        name="token_mixers",
    )(r, r, w_f, w_b, kd_f, kd_b, v, v, b_f, b_b, nkk, nkk, q_t, k, v_t)


def _mixout_kernel(x_ref, o_ref, yf_ref, yb_ref, bonus_ref, g_ref, mod_ref, on_ref, gng_ref,
                   gnb_ref, seg_ref, wo_ref, out_ref):
    D = D_MODEL
    gt1 = mod_ref[0][:, 2 * D:3 * D]
    seg = seg_ref[...]
    y_mla = _rms(o_ref[0]) * on_ref[...]
    y = yf_ref[0] + yb_ref[0]
    inv_n = 1.0 / RWKV_HEAD
    d = y - _seg_dot(y, seg) * inv_n
    var = _seg_dot(d * d, seg) * inv_n
    yn = d * lax.rsqrt(var + GN_EPS) * gng_ref[...] + gnb_ref[...]
    y_rwkv = (yn + bonus_ref[0]) * g_ref[0]
    proj = (jnp.dot(y_mla.astype(BF16), wo_ref[0:MLA_WIDTH, :], preferred_element_type=F32)
            + jnp.dot(y_rwkv.astype(BF16), wo_ref[MLA_WIDTH:, :], preferred_element_type=F32))
    out_ref[0] = x_ref[0] + gt1 * proj


def _mixout(x, o, y_f, y_b, bonus, g, mod_l, lw, tm):
    B, S, D = x.shape
    W = RWKV_WIDTH

    def full(a):
        return pl.BlockSpec(a.shape, lambda b, i: (0,) * a.ndim)

    weights = [lw["out_norm"], lw["gn_g"], lw["gn_b"], lw["seg"], lw["w_out"]]
    tokw = pl.BlockSpec((1, tm, W), lambda b, i: (b, i, 0))
    tokd = pl.BlockSpec((1, tm, D), lambda b, i: (b, i, 0))
    return pl.pallas_call(
        _mixout_kernel,
        grid=(B, S // tm),
        in_specs=[tokd, tokw, tokw, tokw, tokw, tokw,
                  pl.BlockSpec((1, 1, 6 * D), lambda b, i: (b, 0, 0))] + [full(a) for a in weights],
        out_specs=tokd,
        out_shape=jax.ShapeDtypeStruct((B, S, D), F32),
        compiler_params=pltpu.CompilerParams(
            dimension_semantics=("arbitrary", "arbitrary"), vmem_limit_bytes=VMEM_LIMIT),
        name="mixer_out",
    )(x, o, y_f, y_b, bonus, g, mod_l, *weights)


def _ffn_kernel(xm_ref, xp_ref, xn_ref, mod_ref, wg_ref, wv_ref, cw_ref, cb_ref, wd_ref,
                out_ref, h_ref, acc_ref, *, tm, nt, nc):
    i = pl.program_id(1)
    c = pl.program_id(2)
    D = D_MODEL
    te = tm + 2 * HALO
    mod = mod_ref[0]

    @pl.when(c == 0)
    def _():
        x_ext = jnp.concatenate([xp_ref[0], xm_ref[0], xn_ref[0]], axis=0)
        sh2 = mod[:, 3 * D:4 * D]
        sc2 = mod[:, 4 * D:5 * D]
        h_ref[...] = (_rms(x_ext) * (1.0 + sc2) + sh2).astype(BF16)
        acc_ref[...] = jnp.zeros_like(acc_ref)

    h = h_ref[...]
    gate = jnp.dot(h, wg_ref[...], preferred_element_type=F32)
    val = jnp.dot(h[HALO:HALO + tm], wv_ref[...], preferred_element_type=F32)
    row = lax.broadcasted_iota(jnp.int32, (te, 1), 0)
    lo, hi = _halo_bounds(i, nt, tm)
    gate = jnp.where((row >= lo) & (row < hi), gate, 0.0)
    gp = pltpu.roll(gate, 1, 0)[HALO:HALO + tm]
    gn = pltpu.roll(gate, te - 1, 0)[HALO:HALO + tm]
    gc = gate[HALO:HALO + tm]
    cw = cw_ref[...]
    gg = cw[0:1] * gp + cw[1:2] * gc + cw[2:3] * gn + cb_ref[...]
    act = gg * _sigmoid(gg) * val
    acc_ref[...] += jnp.dot(act.astype(BF16), wd_ref[...], preferred_element_type=F32)

    @pl.when(c == nc - 1)
    def _():
        gt2 = mod[:, 5 * D:6 * D]
        out_ref[0] = xm_ref[0] + gt2 * acc_ref[...]


def _ffn(x, mod_l, lw, tm, fc):
    B, S, D = x.shape
    nt = S // tm
    nc = D_FF // fc
    hb = tm // HALO
    nhb = S // HALO
    return pl.pallas_call(
        functools.partial(_ffn_kernel, tm=tm, nt=nt, nc=nc),
        grid=(B, nt, nc),
        in_specs=[
            pl.BlockSpec((1, tm, D), lambda b, i, c: (b, i, 0)),
            pl.BlockSpec((1, HALO, D), lambda b, i, c: (b, jnp.maximum(i * hb - 1, 0), 0)),
            pl.BlockSpec((1, HALO, D), lambda b, i, c: (b, jnp.minimum((i + 1) * hb, nhb - 1), 0)),
            pl.BlockSpec((1, 1, 6 * D), lambda b, i, c: (b, 0, 0)),
            pl.BlockSpec((D, fc), lambda b, i, c: (0, c)),
            pl.BlockSpec((D, fc), lambda b, i, c: (0, nc + c)),
            pl.BlockSpec((3, fc), lambda b, i, c: (0, c)),
            pl.BlockSpec((1, fc), lambda b, i, c: (0, c)),
            pl.BlockSpec((fc, D), lambda b, i, c: (c, 0)),
        ],
        out_specs=pl.BlockSpec((1, tm, D), lambda b, i, c: (b, i, 0)),
        out_shape=jax.ShapeDtypeStruct((B, S, D), F32),
        scratch_shapes=[pltpu.VMEM((tm + 2 * HALO, D), BF16), pltpu.VMEM((tm, D), F32)],
        compiler_params=pltpu.CompilerParams(
            dimension_semantics=("arbitrary", "arbitrary", "arbitrary"),
            vmem_limit_bytes=VMEM_LIMIT),
        name="conv_ffn",
    )(x, x, x, mod_l, lw["w_up"], lw["w_up"], lw["conv_w"], lw["conv_b"], lw["w_down"])


def _final_norm_kernel(x_ref, g_ref, o_ref):
    o_ref[0] = _rms(x_ref[0]) * g_ref[...]


def _final_norm(x, gain, tm):
    B, S, D = x.shape
    tok = pl.BlockSpec((1, tm, D), lambda b, i: (b, i, 0))
    return pl.pallas_call(
        _final_norm_kernel,
        grid=(B, S // tm),
        in_specs=[tok, pl.BlockSpec((1, D), lambda b, i: (0, 0))],
        out_specs=tok,
        out_shape=jax.ShapeDtypeStruct((B, S, D), F32),
        compiler_params=pltpu.CompilerParams(
            dimension_semantics=("arbitrary", "arbitrary"), vmem_limit_bytes=VMEM_LIMIT),
        name="final_norm",
    )(x, gain.reshape(1, D))


def _block_diag2(a, b):
    za = jnp.zeros((a.shape[0], b.shape[1]), a.dtype)
    zb = jnp.zeros((b.shape[0], a.shape[1]), a.dtype)
    return jnp.concatenate([jnp.concatenate([a, za], axis=1),
                            jnp.concatenate([zb, b], axis=1)], axis=0)


def _swap_halves(cols):
    half = cols.shape[-1] // 2
    return jnp.concatenate([cols[..., half:], cols[..., :half]], axis=-1)


def _layer_weights(l, p):
    W = RWKV_WIDTH
    w_in = p["w_in"][l]
    mla = w_in[:, :448]
    rw = w_in[:, 448:]
    kpe = mla[:, 384:448]
    rw_perm = jnp.concatenate([rw[:, :3 * W], rw[:, 3 * W + 256:], rw[:, 3 * W:3 * W + 256]], axis=1)
    mu = p["rwkv_mu"][l]
    mu_perm = jnp.concatenate([mu[:, :3 * W], mu[:, 3 * W + 256:], mu[:, 3 * W:3 * W + 256]], axis=1)
    w_in_perm = jnp.concatenate([mla, _swap_halves(kpe), rw_perm], axis=1).astype(BF16)

    w_uq = p["mla_w_uq"][l].reshape(MLA_Q_RANK, MLA_HEADS, MLA_QK)
    pe = w_uq[:, :, MLA_NOPE:]
    w_uq_t = jnp.concatenate([w_uq, _swap_halves(pe)], axis=-1).reshape(
        MLA_Q_RANK, MLA_HEADS * Q_HEAD_ROWS).T.astype(BF16)
    w_ukv = p["mla_w_ukv"][l].reshape(MLA_KV_RANK, MLA_HEADS, MLA_NOPE + MLA_V)
    w_uk = w_ukv[:, :, :MLA_NOPE].reshape(MLA_KV_RANK, MLA_HEADS * MLA_NOPE).astype(BF16)
    w_uv_t = w_ukv[:, :, MLA_NOPE:].reshape(MLA_KV_RANK, MLA_HEADS * MLA_V).T.astype(BF16)

    head = jnp.arange(W) // RWKV_HEAD
    seg = (head[:, None] == head[None, :]).astype(BF16)
    return {
        "w_in": w_in_perm,
        "mu": mu_perm,
        "q_norm": p["mla_q_norm"][l].reshape(1, -1),
        "w_uq_t": w_uq_t,
        "kv_norm": p["mla_kv_norm"][l].reshape(1, -1),
        "w_uk": w_uk,
        "w_uv_t": w_uv_t,
        "out_norm": p["mla_out_norm"][l].reshape(1, -1),
        "w0": p["rwkv_w0"][l].reshape(1, 2 * W),
        "w2": _block_diag2(p["rwkv_w2"][l, 0], p["rwkv_w2"][l, 1]).astype(BF16),
        "a0": p["rwkv_a0"][l].reshape(1, 2 * W),
        "a2": _block_diag2(p["rwkv_a2"][l, 0], p["rwkv_a2"][l, 1]).astype(BF16),
        "g2": p["rwkv_g2"][l].astype(BF16),
        "k_k": p["rwkv_k_k"][l].reshape(1, W),
        "k_a": p["rwkv_k_a"][l].reshape(1, W),
        "r_k": p["rwkv_r_k"][l].reshape(1, W),
        "gn_g": p["rwkv_gn_g"][l].reshape(1, W),
        "gn_b": p["rwkv_gn_b"][l].reshape(1, W),
        "seg": seg,
        "w_out": p["w_out"][l].astype(BF16),
        "w_up": p["ffn_w_up"][l].astype(BF16),
        "conv_w": p["ffn_conv_w"][l],
        "conv_b": p["ffn_conv_b"][l].reshape(1, D_FF),
        "w_down": p["ffn_w_down"][l].astype(BF16),
    }


def _forward(x, c, positions, p, cfg):
    B, S, D = x.shape
    mod = _adaln(c, p["ada_w"], p["ada_b"])
    rope_t, rope = _rope_tables(positions, cfg["tm_in"])
    for l in range(DEPTH):
        lw = _layer_weights(l, p)
        mod_l = mod[l].reshape(B, 1, 6 * D)
        (q_t, k, v_t, r, vv, nkk, w_f, w_b, kd_f, kd_b, b_f, b_b, bonus, g) = _inproj(
            x, mod_l, rope, rope_t, lw, cfg["tm_in"])
        y_f, y_b, o = _mixers(r, w_f, w_b, kd_f, kd_b, vv, b_f, b_b, nkk, q_t, k, v_t,
                              cfg["tb"], cfg["tk"])
        x = _mixout(x, o, y_f, y_b, bonus, g, mod_l, lw, cfg["tm_out"])
        x = _ffn(x, mod_l, lw, cfg["tm_ffn"], cfg["fc"])
    return _final_norm(x, p["final_norm"], cfg["tm_out"])


_CFG = {"tm_in": 256, "tk": 512, "tb": 64, "tm_out": 512, "tm_ffn": 512, "fc": 1408}


def kernel(x, c, positions, ada_w, ada_b, w_in, mla_q_norm, mla_w_uq, mla_kv_norm, mla_w_ukv, mla_out_norm, rwkv_mu, rwkv_w0, rwkv_w2, rwkv_a0, rwkv_a2, rwkv_g2, rwkv_k_k, rwkv_k_a, rwkv_r_k, rwkv_gn_g, rwkv_gn_b, w_out, ffn_w_up, ffn_conv_w, ffn_conv_b, ffn_w_down, final_norm):
    p = dict(ada_w=ada_w, ada_b=ada_b, w_in=w_in, mla_q_norm=mla_q_norm, mla_w_uq=mla_w_uq,
             mla_kv_norm=mla_kv_norm, mla_w_ukv=mla_w_ukv, mla_out_norm=mla_out_norm,
             rwkv_mu=rwkv_mu, rwkv_w0=rwkv_w0, rwkv_w2=rwkv_w2, rwkv_a0=rwkv_a0, rwkv_a2=rwkv_a2,
             rwkv_g2=rwkv_g2, rwkv_k_k=rwkv_k_k, rwkv_k_a=rwkv_k_a, rwkv_r_k=rwkv_r_k,
             rwkv_gn_g=rwkv_gn_g, rwkv_gn_b=rwkv_gn_b, w_out=w_out, ffn_w_up=ffn_w_up,
             ffn_conv_w=ffn_conv_w, ffn_conv_b=ffn_conv_b, ffn_w_down=ffn_w_down,
             final_norm=final_norm)
    return _forward(x, c, positions, p, _CFG)
```

```python
import functools

import jax
import jax.numpy as jnp
from jax import lax
from jax.experimental import pallas as pl
from jax.experimental.pallas import tpu as pltpu

D_MODEL = 1024
DEPTH = 4
MLA_HEADS = 4
MLA_NOPE = 128
MLA_ROPE = 64
MLA_V = 128
MLA_Q_RANK = 256
MLA_KV_RANK = 128
MLA_WIDTH = MLA_HEADS * MLA_V
MLA_QK = MLA_NOPE + MLA_ROPE
ROPE_BASE = 10000.0
RWKV_WIDTH = D_MODEL - MLA_WIDTH
RWKV_HEAD = 64
RWKV_HEADS = RWKV_WIDTH // RWKV_HEAD
DECAY_LORA = 64
AAA_LORA = 64
GATE_LORA = 128
GN_EPS = 64e-5
D_FF = 2816
EPS = 1e-6

MLA_COLS = 512
RWKV_COLS = 3 * RWKV_WIDTH + GATE_LORA + 2 * DECAY_LORA + 2 * AAA_LORA
IN_COLS = MLA_COLS + RWKV_COLS
Q_HEAD_ROWS = MLA_NOPE + 2 * MLA_ROPE
BF16_ROWS = 16
V_ROWS = MLA_V + BF16_ROWS

LANES = 128
HALO = 8
VMEM_LIMIT = 56 * 1024 * 1024
LOG2E = 1.4426950408889634
DECAY_SCALE = 0.6065306597126334

F32 = jnp.float32
BF16 = jnp.bfloat16
NT_DIMS = (((1,), (1,)), ((), ()))


def _sigmoid(x):
    return 1.0 / (1.0 + jnp.exp(-x))


def _rms(x, eps=EPS):
    return x * lax.rsqrt(jnp.mean(x * x, axis=-1, keepdims=True) + eps)


def _seg_dot(x, seg):
    hi = x.astype(BF16)
    lo = (x - hi.astype(F32)).astype(BF16)
    return (jnp.dot(hi, seg, preferred_element_type=F32)
            + jnp.dot(lo, seg, preferred_element_type=F32))


def _halo_bounds(i, nt, tm):
    lo = jnp.where(i > 0, 0, HALO)
    hi = jnp.where(i < nt - 1, tm + 2 * HALO, tm + HALO)
    return lo, hi


def _adaln_kernel(c_ref, w_ref, b_ref, o_ref):
    c = c_ref[...]
    ca = c * _sigmoid(c)
    o_ref[0] = jnp.dot(ca, w_ref[0], precision=lax.Precision.HIGHEST,
                       preferred_element_type=F32) + b_ref[0]


def _adaln(c, ada_w, ada_b):
    L, D, D6 = ada_w.shape
    B = c.shape[0]
    nj = D6 // D
    return pl.pallas_call(
        _adaln_kernel,
        grid=(L, nj),
        in_specs=[
            pl.BlockSpec((B, D), lambda l, j: (0, 0)),
            pl.BlockSpec((1, D, D), lambda l, j: (l, 0, j)),
            pl.BlockSpec((1, 1, D), lambda l, j: (l, 0, j)),
        ],
        out_specs=pl.BlockSpec((1, B, D), lambda l, j: (l, 0, j)),
        out_shape=jax.ShapeDtypeStruct((L, B, D6), F32),
        compiler_params=pltpu.CompilerParams(
            dimension_semantics=("arbitrary", "arbitrary"), vmem_limit_bytes=VMEM_LIMIT),
        name="adaln",
    )(c, ada_w, ada_b.reshape(L, 1, D6))


def _rope_kernel(pos_ref, freq_ref, tab_t_ref, tab_ref):
    ang = freq_ref[...] * pos_ref[0].astype(F32)
    row = lax.broadcasted_iota(jnp.int32, ang.shape, 0)
    cos = jnp.cos(ang)
    sin = jnp.sin(ang)
    tab_t = jnp.where(row < 64, cos, jnp.where(row < 96, -sin, sin))
    tab_t_ref[0] = tab_t
    tab_ref[0] = tab_t.T


def _rope_tables(positions, ts):
    B, S = positions.shape
    half = MLA_ROPE // 2
    inv_freq = ROPE_BASE ** (-jnp.arange(half, dtype=F32) / half)
    freq = jnp.tile(inv_freq, 4).reshape(LANES, 1)
    return pl.pallas_call(
        _rope_kernel,
        grid=(B, S // ts),
        in_specs=[
            pl.BlockSpec((1, 1, ts), lambda b, i: (b, 0, i)),
            pl.BlockSpec((LANES, 1), lambda b, i: (0, 0)),
        ],
        out_specs=(pl.BlockSpec((1, LANES, ts), lambda b, i: (b, 0, i)),
                   pl.BlockSpec((1, ts, LANES), lambda b, i: (b, i, 0))),
        out_shape=(jax.ShapeDtypeStruct((B, LANES, S), F32),
                   jax.ShapeDtypeStruct((B, S, LANES), F32)),
        compiler_params=pltpu.CompilerParams(
            dimension_semantics=("arbitrary", "arbitrary"), vmem_limit_bytes=VMEM_LIMIT),
        name="rope_tables",
    )(positions.reshape(B, 1, S), freq)


def _inproj_kernel(xm_ref, xp_ref, xn_ref, mod_ref, rope_ref, rope_t_ref, w_in_ref, mu_ref,
                   qn_ref, wuq_t_ref, kvn_ref, wuk_ref, wuv_t_ref,
                   w0_ref, w2_ref, a0_ref, a2_ref, g2_ref, kk_ref, ka_ref, rk_ref, seg_ref,
                   qt_out, k_out, vt_out, r_out, vv_out, nkk_out, wf_out, wb_out,
                   kdf_out, kdb_out, bf_out, bb_out, bonus_out, g_out, *, tm, nt):
    i = pl.program_id(1)
    D = D_MODEL
    W = RWKV_WIDTH
    te = tm + 2 * HALO
    x_ext = jnp.concatenate([xp_ref[0], xm_ref[0], xn_ref[0]], axis=0)
    mod = mod_ref[0]
    sh1 = mod[:, 0:D]
    sc1 = mod[:, D:2 * D]
    h = _rms(x_ext) * (1.0 + sc1) + sh1
    z = jnp.dot(h.astype(BF16), w_in_ref[...], preferred_element_type=F32)

    zr = z[:, MLA_COLS:]
    row = lax.broadcasted_iota(jnp.int32, (te, 1), 0)
    lo, hi = _halo_bounds(i, nt, tm)
    zr = jnp.where((row >= lo) & (row < hi), zr, 0.0)
    zc = zr[HALO:HALO + tm]
    zp = pltpu.roll(zr, 1, 0)[HALO:HALO + tm]
    zn = pltpu.roll(zr, te - 1, 0)[HALO:HALO + tm]
    mu = mu_ref[...]
    zs = zc + mu[0:1] * (zp - zc) + mu[1:2] * (zn - zc)

    zm = z[HALO:HALO + tm, 0:MLA_COLS]
    hq = (_rms(zm[:, 0:MLA_Q_RANK]) * qn_ref[...]).astype(BF16)
    q_t = lax.dot_general(wuq_t_ref[...], hq, NT_DIMS, preferred_element_type=F32)
    hkv = (_rms(zm[:, MLA_Q_RANK:MLA_Q_RANK + MLA_KV_RANK]) * kvn_ref[...]).astype(BF16)
    kn = jnp.dot(hkv, wuk_ref[...], preferred_element_type=F32)
    v_t = lax.dot_general(wuv_t_ref[...], hkv, NT_DIMS, preferred_element_type=F32)
    kprod = zm[:, 384:512] * rope_ref[0]
    kpe = (kprod + pltpu.roll(kprod, 64, 1))[:, 0:MLA_ROPE].astype(BF16)
    rope_t = rope_t_ref[0]
    qscale = (MLA_QK ** -0.5) * LOG2E
    for hd in range(MLA_HEADS):
        qh = q_t[hd * Q_HEAD_ROWS:(hd + 1) * Q_HEAD_ROWS]
        qprod = qh[MLA_NOPE:] * rope_t
        qpe = qprod[0:MLA_ROPE] + qprod[MLA_ROPE:]
        qt_out[0, hd, 0:MLA_NOPE, :] = (qh[0:MLA_NOPE] * qscale).astype(BF16)
        qt_out[0, hd, MLA_NOPE:MLA_QK, :] = (qpe * qscale).astype(BF16)
        k_out[0, hd, :, 0:MLA_NOPE] = kn[:, hd * MLA_NOPE:(hd + 1) * MLA_NOPE].astype(BF16)
        k_out[0, hd, :, MLA_NOPE:MLA_QK] = kpe
        vt_out[0, hd, 0:MLA_V, :] = v_t[hd * MLA_V:(hd + 1) * MLA_V].astype(BF16)
        vt_out[0, hd, MLA_V:V_ROWS, :] = jnp.ones((BF16_ROWS, tm), BF16)

    r = zs[:, 0:W]
    k = zs[:, W:2 * W]
    v = zs[:, 2 * W:3 * W]
    xg = zs[:, 3 * W:3 * W + GATE_LORA]
    xw = zs[:, 3 * W + GATE_LORA:3 * W + GATE_LORA + 2 * DECAY_LORA]
    xa = zs[:, 3 * W + GATE_LORA + 2 * DECAY_LORA:]
    seg = seg_ref[...]
    g_out[0] = jnp.dot(_sigmoid(xg).astype(BF16), g2_ref[...], preferred_element_type=F32)
    kk = k * kk_ref[...]
    kkn = kk * lax.rsqrt(_seg_dot(kk * kk, seg) + 1e-12)
    lw = jnp.dot(jnp.tanh(xw).astype(BF16), w2_ref[...], preferred_element_type=F32) + w0_ref[...]
    w = jnp.exp(_sigmoid(lw) * (-DECAY_SCALE))
    a = _sigmoid(jnp.dot(xa.astype(BF16), a2_ref[...], preferred_element_type=F32) + a0_ref[...])
    ka = ka_ref[...]
    kd_f = k * (1.0 + (a[:, 0:W] - 1.0) * ka)
    kd_b = k * (1.0 + (a[:, W:] - 1.0) * ka)
    r_out[0] = r
    vv_out[0] = v
    nkk_out[0] = -kkn
    wf_out[0] = w[:, 0:W]
    wb_out[0] = w[:, W:]
    kdf_out[0] = kd_f
    kdb_out[0] = kd_b
    bf_out[0] = kkn * a[:, 0:W]
    bb_out[0] = kkn * a[:, W:]
    bonus_out[0] = _seg_dot(r * (0.5 * (kd_f + kd_b)) * rk_ref[...], seg) * v


def _inproj(x, mod_l, rope, rope_t, lw, tm):
    B, S, D = x.shape
    nt = S // tm
    hb = tm // HALO
    nhb = S // HALO
    W = RWKV_WIDTH
    H = MLA_HEADS

    def full(a):
        return pl.BlockSpec(a.shape, lambda b, i: (0,) * a.ndim)

    weights = [lw["w_in"], lw["mu"], lw["q_norm"], lw["w_uq_t"], lw["kv_norm"], lw["w_uk"],
               lw["w_uv_t"], lw["w0"], lw["w2"], lw["a0"], lw["a2"], lw["g2"], lw["k_k"],
               lw["k_a"], lw["r_k"], lw["seg"]]
    tok = pl.BlockSpec((1, tm, W), lambda b, i: (b, i, 0))
    tok_shape = jax.ShapeDtypeStruct((B, S, W), F32)
    out_shape = (
        jax.ShapeDtypeStruct((B, H, MLA_QK, S), BF16),
        jax.ShapeDtypeStruct((B, H, S, MLA_QK), BF16),
        jax.ShapeDtypeStruct((B, H, V_ROWS, S), BF16),
    ) + (tok_shape,) * 11
    out_specs = (
        pl.BlockSpec((1, H, MLA_QK, tm), lambda b, i: (b, 0, 0, i)),
        pl.BlockSpec((1, H, tm, MLA_QK), lambda b, i: (b, 0, i, 0)),
        pl.BlockSpec((1, H, V_ROWS, tm), lambda b, i: (b, 0, 0, i)),
    ) + (tok,) * 11
    return pl.pallas_call(
        functools.partial(_inproj_kernel, tm=tm, nt=nt),
        grid=(B, nt),
        in_specs=[
            pl.BlockSpec((1, tm, D), lambda b, i: (b, i, 0)),
            pl.BlockSpec((1, HALO, D), lambda b, i: (b, jnp.maximum(i * hb - 1, 0), 0)),
            pl.BlockSpec((1, HALO, D), lambda b, i: (b, jnp.minimum((i + 1) * hb, nhb - 1), 0)),
            pl.BlockSpec((1, 1, 6 * D), lambda b, i: (b, 0, 0)),
            pl.BlockSpec((1, tm, LANES), lambda b, i: (b, i, 0)),
            pl.BlockSpec((1, LANES, tm), lambda b, i: (b, 0, i)),
        ] + [full(a) for a in weights],
        out_specs=out_specs,
        out_shape=out_shape,
        compiler_params=pltpu.CompilerParams(
            dimension_semantics=("arbitrary", "arbitrary"), vmem_limit_bytes=VMEM_LIMIT),
        name="inproj",
    )(x, x, x, mod_l, rope, rope_t, *weights)


SCAN_RING = 4
_R, _W, _KD, _V, _B, _NK = range(6)


def _mixers_kernel(rf_ref, rb_ref, wf_ref, wb_ref, kdf_ref, kdb_ref, vf_ref, vb_ref, bf_ref, bb_ref,
                   nkf_ref, nkb_ref, qt_ref, k_ref, vt_ref, yf_ref, yb_ref, o_ref, *scratch,
                   tb, tq, tk, nk):
    N = RWKV_HEAD
    HP = RWKV_HEADS // 2
    B = rf_ref.shape[0]
    G = HP * B
    slabs = scratch[0:SCAN_RING]
    y_ring = scratch[SCAN_RING:2 * SCAN_RING]
    state_ref, sa_ref, s_ref, cmax_ref, acc_ref, m_ref = scratch[2 * SCAN_RING:]
    C = sa_ref.shape[-1]
    pairs = ((rf_ref, rb_ref), (wf_ref, wb_ref), (kdf_ref, kdb_ref), (vf_ref, vb_ref),
             (bf_ref, bb_ref), (nkf_ref, nkb_ref))

    def fill(slot, j):
        for idx, (f_ref, b_ref) in enumerate(pairs):
            f_row = f_ref[:, j, :]
            b_row = b_ref[:, tb - 1 - j, :]
            rows = ([f_row[:, hp * LANES:(hp + 1) * LANES] for hp in range(HP)]
                    + [b_row[:, hp * LANES:(hp + 1) * LANES] for hp in range(HP)])
            t = jnp.concatenate(rows, axis=0).T
            slabs[slot][idx] = jnp.concatenate([t[0:N], t[N:2 * N]], axis=1)

    def scores(grp, c, slot):
        q_t = qt_ref[0, 0, :, pl.ds(pl.multiple_of(grp * tq, tq), tq)]
        start = pl.multiple_of(c * tk, tk)
        s_t = jnp.dot(k_ref[0, 0, pl.ds(start, tk), :], q_t,
                      preferred_element_type=F32)
        s_ref[slot] = s_t
        cmax_ref[slot] = jnp.max(s_t, axis=0, keepdims=True)

    def consume(c, slot):
        start = pl.multiple_of(c * tk, tk)
        s_t = s_ref[slot]
        m = m_ref[...]
        m_new = jnp.maximum(m, cmax_ref[slot])
        p_t = jnp.exp2(s_t - m_new).astype(BF16)
        alpha = jnp.exp2(m - m_new)
        acc_ref[...] = acc_ref[...] * alpha + jnp.dot(
            vt_ref[0, 0, :, pl.ds(start, tk)], p_t, preferred_element_type=F32)
        m_ref[...] = m_new

    def emit(slot, j):
        y = y_ring[slot][...]
        t = jnp.concatenate([y[:, 0:2 * G], y[:, 2 * G:C]], axis=0).T
        yf_ref[:, j, :] = jnp.concatenate([t[hp * B:(hp + 1) * B] for hp in range(HP)], axis=1)
        yb_ref[:, tb - 1 - j, :] = jnp.concatenate(
            [t[G + hp * B:G + (hp + 1) * B] for hp in range(HP)], axis=1)

    @pl.when(pl.program_id(0) == 0)
    def _():
        state_ref[...] = jnp.zeros_like(state_ref)
        for y_ref in y_ring:
            y_ref[...] = jnp.zeros_like(y_ref)

    fill(0, 0)
    fill(1, 1)
    sa0 = jnp.zeros((N, C), F32)
    for kc in range(N):
        sa0 = sa0 + state_ref[kc] * slabs[0][_NK, pl.ds(kc, 1), :]
    sa_ref[...] = sa0

    def step(grp, c, cur):
        j = grp * nk + c
        nxt = (cur + 1) % SCAN_RING
        fill((cur + 2) % SCAN_RING, jnp.minimum(j + 2, tb - 1))
        emit((cur + SCAN_RING - 1) % SCAN_RING, jnp.maximum(j - 1, 0))
        scores(grp, jnp.minimum(c + 1, nk - 1), (cur + 1) % 2)
        consume(c, cur % 2)
        for hv in range(4):
            rows = slice(hv * (N // 4), (hv + 1) * (N // 4))
            sa = sa_ref[rows, :]
            vv = slabs[cur][_V, rows, :]
            y = jnp.zeros_like(sa)
            sa_next = jnp.zeros_like(sa)
            for kc in range(N):
                row = pl.ds(kc, 1)
                s_new = (state_ref[kc, rows, :] * slabs[cur][_W, row, :]
                         + sa * slabs[cur][_B, row, :] + vv * slabs[cur][_KD, row, :])
                state_ref[kc, rows, :] = s_new
                y = y + s_new * slabs[cur][_R, row, :]
                sa_next = sa_next + s_new * slabs[nxt][_NK, row, :]
            sa_ref[rows, :] = sa_next
            y_ring[cur][rows, :] = y

    def group(grp, _):
        m_ref[...] = jnp.full(m_ref.shape, -jnp.inf, F32)
        acc_ref[...] = jnp.zeros_like(acc_ref)
        scores(grp, 0, 0)

        def body(i, _):
            for u in range(SCAN_RING):
                pl.when(i >= -u)(functools.partial(step, grp, i * SCAN_RING + u, u))
            return 0

        lax.fori_loop(0, nk // SCAN_RING, body, 0)
        acc = acc_ref[...]
        o_ref[0, pl.ds(pl.multiple_of(grp * tq, tq), tq), :] = (
            acc[0:MLA_V] / acc[MLA_V:MLA_V + 1]).T
        return 0

    lax.fori_loop(0, tb // nk, group, 0)
    emit((tb - 1) % SCAN_RING, tb - 1)


def _mixers(r, w_f, w_b, kd_f, kd_b, v, b_f, b_b, nkk, q_t, k, v_t, tb, tk):
    B, S, W = r.shape
    H = MLA_HEADS
    nb = S // tb
    nk = S // tk
    gps = tb // nk
    tq = B * H * tb // gps
    qblk = S // (gps * tq)
    assert tb % nk == 0 and nk % SCAN_RING == 0 and S % (gps * tq) == 0 and nb == B * H * qblk
    chains = 2 * RWKV_HEADS * B
    fwd = pl.BlockSpec((B, tb, W), lambda i: (0, i, 0))
    bwd = pl.BlockSpec((B, tb, W), lambda i: (0, nb - 1 - i, 0))

    def bh(i):
        return i // (H * qblk), (i // qblk) % H

    return pl.pallas_call(
        functools.partial(_mixers_kernel, tb=tb, tq=tq, tk=tk, nk=nk),
        grid=(nb,),
        in_specs=[fwd, bwd] * 6 + [
            pl.BlockSpec((1, 1, MLA_QK, gps * tq), lambda i: (*bh(i), 0, i % qblk)),
            pl.BlockSpec((1, 1, S, MLA_QK), lambda i: (*bh(i), 0, 0)),
            pl.BlockSpec((1, 1, V_ROWS, S), lambda i: (*bh(i), 0, 0)),
        ],
        out_specs=(fwd, bwd,
                   pl.BlockSpec((1, gps * tq, MLA_V), lambda i: (bh(i)[0], i % qblk, bh(i)[1]))),
        out_shape=(jax.ShapeDtypeStruct((B, S, W), F32),) * 2
        + (jax.ShapeDtypeStruct((B, S, H * MLA_V), F32),),
        scratch_shapes=[pltpu.VMEM((6, RWKV_HEAD, chains), F32)] * SCAN_RING
        + [pltpu.VMEM((RWKV_HEAD, chains), F32)] * SCAN_RING
        + [pltpu.VMEM((RWKV_HEAD, RWKV_HEAD, chains), F32),
           pltpu.VMEM((RWKV_HEAD, chains), F32),
           pltpu.VMEM((2, tk, tq), F32), pltpu.VMEM((2, 1, tq), F32),
           pltpu.VMEM((V_ROWS, tq), F32), pltpu.VMEM((1, tq), F32)],
        compiler_params=pltpu.CompilerParams(
            dimension_semantics=("arbitrary",), vmem_limit_bytes=VMEM_LIMIT),
        name="token_mixers",
    )(r, r, w_f, w_b, kd_f, kd_b, v, v, b_f, b_b, nkk, nkk, q_t, k, v_t)


def _mixout_kernel(x_ref, o_ref, yf_ref, yb_ref, bonus_ref, g_ref, mod_ref, on_ref, gng_ref,
                   gnb_ref, seg_ref, wo_ref, out_ref):
    D = D_MODEL
    gt1 = mod_ref[0][:, 2 * D:3 * D]
    seg = seg_ref[...]
    y_mla = _rms(o_ref[0]) * on_ref[...]
    y = yf_ref[0] + yb_ref[0]
    inv_n = 1.0 / RWKV_HEAD
    d = y - _seg_dot(y, seg) * inv_n
    var = _seg_dot(d * d, seg) * inv_n
    yn = d * lax.rsqrt(var + GN_EPS) * gng_ref[...] + gnb_ref[...]
    y_rwkv = (yn + bonus_ref[0]) * g_ref[0]
    proj = (jnp.dot(y_mla.astype(BF16), wo_ref[0:MLA_WIDTH, :], preferred_element_type=F32)
            + jnp.dot(y_rwkv.astype(BF16), wo_ref[MLA_WIDTH:, :], preferred_element_type=F32))
    out_ref[0] = x_ref[0] + gt1 * proj


def _mixout(x, o, y_f, y_b, bonus, g, mod_l, lw, tm):
    B, S, D = x.shape
    W = RWKV_WIDTH

    def full(a):
        return pl.BlockSpec(a.shape, lambda b, i: (0,) * a.ndim)

    weights = [lw["out_norm"], lw["gn_g"], lw["gn_b"], lw["seg"], lw["w_out"]]
    tokw = pl.BlockSpec((1, tm, W), lambda b, i: (b, i, 0))
    tokd = pl.BlockSpec((1, tm, D), lambda b, i: (b, i, 0))
    return pl.pallas_call(
        _mixout_kernel,
        grid=(B, S // tm),
        in_specs=[tokd, tokw, tokw, tokw, tokw, tokw,
                  pl.BlockSpec((1, 1, 6 * D), lambda b, i: (b, 0, 0))] + [full(a) for a in weights],
        out_specs=tokd,
        out_shape=jax.ShapeDtypeStruct((B, S, D), F32),
        compiler_params=pltpu.CompilerParams(
            dimension_semantics=("arbitrary", "arbitrary"), vmem_limit_bytes=VMEM_LIMIT),
        name="mixer_out",
    )(x, o, y_f, y_b, bonus, g, mod_l, *weights)


def _ffn_kernel(xm_ref, xp_ref, xn_ref, mod_ref, wg_ref, wv_ref, cw_ref, cb_ref, wd_ref,
                out_ref, h_ref, acc_ref, *, tm, nt, nc):
    i = pl.program_id(1)
    c = pl.program_id(2)
    D = D_MODEL
    te = tm + 2 * HALO
    mod = mod_ref[0]

    @pl.when(c == 0)
    def _():
        x_ext = jnp.concatenate([xp_ref[0], xm_ref[0], xn_ref[0]], axis=0)
        sh2 = mod[:, 3 * D:4 * D]
        sc2 = mod[:, 4 * D:5 * D]
        h_ref[...] = (_rms(x_ext) * (1.0 + sc2) + sh2).astype(BF16)
        acc_ref[...] = jnp.zeros_like(acc_ref)

    h = h_ref[...]
    gate = jnp.dot(h, wg_ref[...], preferred_element_type=F32)
    val = jnp.dot(h[HALO:HALO + tm], wv_ref[...], preferred_element_type=F32)
    row = lax.broadcasted_iota(jnp.int32, (te, 1), 0)
    lo, hi = _halo_bounds(i, nt, tm)
    gate = jnp.where((row >= lo) & (row < hi), gate, 0.0)
    gp = pltpu.roll(gate, 1, 0)[HALO:HALO + tm]
    gn = pltpu.roll(gate, te - 1, 0)[HALO:HALO + tm]
    gc = gate[HALO:HALO + tm]
    cw = cw_ref[...]
    gg = cw[0:1] * gp + cw[1:2] * gc + cw[2:3] * gn + cb_ref[...]
    act = gg * _sigmoid(gg) * val
    acc_ref[...] += jnp.dot(act.astype(BF16), wd_ref[...], preferred_element_type=F32)

    @pl.when(c == nc - 1)
    def _():
        gt2 = mod[:, 5 * D:6 * D]
        out_ref[0] = xm_ref[0] + gt2 * acc_ref[...]


def _ffn(x, mod_l, lw, tm, fc):
    B, S, D = x.shape
    nt = S // tm
    nc = D_FF // fc
    hb = tm // HALO
    nhb = S // HALO
    return pl.pallas_call(
        functools.partial(_ffn_kernel, tm=tm, nt=nt, nc=nc),
        grid=(B, nt, nc),
        in_specs=[
            pl.BlockSpec((1, tm, D), lambda b, i, c: (b, i, 0)),
            pl.BlockSpec((1, HALO, D), lambda b, i, c: (b, jnp.maximum(i * hb - 1, 0), 0)),
            pl.BlockSpec((1, HALO, D), lambda b, i, c: (b, jnp.minimum((i + 1) * hb, nhb - 1), 0)),
            pl.BlockSpec((1, 1, 6 * D), lambda b, i, c: (b, 0, 0)),
            pl.BlockSpec((D, fc), lambda b, i, c: (0, c)),
            pl.BlockSpec((D, fc), lambda b, i, c: (0, nc + c)),
            pl.BlockSpec((3, fc), lambda b, i, c: (0, c)),
            pl.BlockSpec((1, fc), lambda b, i, c: (0, c)),
            pl.BlockSpec((fc, D), lambda b, i, c: (c, 0)),
        ],
        out_specs=pl.BlockSpec((1, tm, D), lambda b, i, c: (b, i, 0)),
        out_shape=jax.ShapeDtypeStruct((B, S, D), F32),
        scratch_shapes=[pltpu.VMEM((tm + 2 * HALO, D), BF16), pltpu.VMEM((tm, D), F32)],
        compiler_params=pltpu.CompilerParams(
            dimension_semantics=("arbitrary", "arbitrary", "arbitrary"),
            vmem_limit_bytes=VMEM_LIMIT),
        name="conv_ffn",
    )(x, x, x, mod_l, lw["w_up"], lw["w_up"], lw["conv_w"], lw["conv_b"], lw["w_down"])


def _final_norm_kernel(x_ref, g_ref, o_ref):
    o_ref[0] = _rms(x_ref[0]) * g_ref[...]


def _final_norm(x, gain, tm):
    B, S, D = x.shape
    tok = pl.BlockSpec((1, tm, D), lambda b, i: (b, i, 0))
    return pl.pallas_call(
        _final_norm_kernel,
        grid=(B, S // tm),
        in_specs=[tok, pl.BlockSpec((1, D), lambda b, i: (0, 0))],
        out_specs=tok,
        out_shape=jax.ShapeDtypeStruct((B, S, D), F32),
        compiler_params=pltpu.CompilerParams(
            dimension_semantics=("arbitrary", "arbitrary"), vmem_limit_bytes=VMEM_LIMIT),
        name="final_norm",
    )(x, gain.reshape(1, D))


def _block_diag2(a, b):
    za = jnp.zeros((a.shape[0], b.shape[1]), a.dtype)
    zb = jnp.zeros((b.shape[0], a.shape[1]), a.dtype)
    return jnp.concatenate([jnp.concatenate([a, za], axis=1),
                            jnp.concatenate([zb, b], axis=1)], axis=0)


def _swap_halves(cols):
    half = cols.shape[-1] // 2
    return jnp.concatenate([cols[..., half:], cols[..., :half]], axis=-1)


def _layer_weights(l, p):
    W = RWKV_WIDTH
    w_in = p["w_in"][l]
    mla = w_in[:, :448]
    rw = w_in[:, 448:]
    kpe = mla[:, 384:448]
    rw_perm = jnp.concatenate([rw[:, :3 * W], rw[:, 3 * W + 256:], rw[:, 3 * W:3 * W + 256]], axis=1)
    mu = p["rwkv_mu"][l]
    mu_perm = jnp.concatenate([mu[:, :3 * W], mu[:, 3 * W + 256:], mu[:, 3 * W:3 * W + 256]], axis=1)
    w_in_perm = jnp.concatenate([mla, _swap_halves(kpe), rw_perm], axis=1).astype(BF16)

    w_uq = p["mla_w_uq"][l].reshape(MLA_Q_RANK, MLA_HEADS, MLA_QK)
    pe = w_uq[:, :, MLA_NOPE:]
    w_uq_t = jnp.concatenate([w_uq, _swap_halves(pe)], axis=-1).reshape(
        MLA_Q_RANK, MLA_HEADS * Q_HEAD_ROWS).T.astype(BF16)
    w_ukv = p["mla_w_ukv"][l].reshape(MLA_KV_RANK, MLA_HEADS, MLA_NOPE + MLA_V)
    w_uk = w_ukv[:, :, :MLA_NOPE].reshape(MLA_KV_RANK, MLA_HEADS * MLA_NOPE).astype(BF16)
    w_uv_t = w_ukv[:, :, MLA_NOPE:].reshape(MLA_KV_RANK, MLA_HEADS * MLA_V).T.astype(BF16)

    head = jnp.arange(W) // RWKV_HEAD
    seg = (head[:, None] == head[None, :]).astype(BF16)
    return {
        "w_in": w_in_perm,
        "mu": mu_perm,
        "q_norm": p["mla_q_norm"][l].reshape(1, -1),
        "w_uq_t": w_uq_t,
        "kv_norm": p["mla_kv_norm"][l].reshape(1, -1),
        "w_uk": w_uk,
        "w_uv_t": w_uv_t,
        "out_norm": p["mla_out_norm"][l].reshape(1, -1),
        "w0": p["rwkv_w0"][l].reshape(1, 2 * W),
        "w2": _block_diag2(p["rwkv_w2"][l, 0], p["rwkv_w2"][l, 1]).astype(BF16),
        "a0": p["rwkv_a0"][l].reshape(1, 2 * W),
        "a2": _block_diag2(p["rwkv_a2"][l, 0], p["rwkv_a2"][l, 1]).astype(BF16),
        "g2": p["rwkv_g2"][l].astype(BF16),
        "k_k": p["rwkv_k_k"][l].reshape(1, W),
        "k_a": p["rwkv_k_a"][l].reshape(1, W),
        "r_k": p["rwkv_r_k"][l].reshape(1, W),
        "gn_g": p["rwkv_gn_g"][l].reshape(1, W),
        "gn_b": p["rwkv_gn_b"][l].reshape(1, W),
        "seg": seg,
        "w_out": p["w_out"][l].astype(BF16),
        "w_up": p["ffn_w_up"][l].astype(BF16),
        "conv_w": p["ffn_conv_w"][l],
        "conv_b": p["ffn_conv_b"][l].reshape(1, D_FF),
        "w_down": p["ffn_w_down"][l].astype(BF16),
    }


def _forward(x, c, positions, p, cfg):
    B, S, D = x.shape
    mod = _adaln(c, p["ada_w"], p["ada_b"])
    rope_t, rope = _rope_tables(positions, cfg["tm_in"])
    for l in range(DEPTH):
        lw = _layer_weights(l, p)
        mod_l = mod[l].reshape(B, 1, 6 * D)
        (q_t, k, v_t, r, vv, nkk, w_f, w_b, kd_f, kd_b, b_f, b_b, bonus, g) = _inproj(
            x, mod_l, rope, rope_t, lw, cfg["tm_in"])
        y_f, y_b, o = _mixers(r, w_f, w_b, kd_f, kd_b, vv, b_f, b_b, nkk, q_t, k, v_t,
                              cfg["tb"], cfg["tk"])
        x = _mixout(x, o, y_f, y_b, bonus, g, mod_l, lw, cfg["tm_out"])
        x = _ffn(x, mod_l, lw, cfg["tm_ffn"], cfg["fc"])
    return _final_norm(x, p["final_norm"], cfg["tm_out"])


_CFG = {"tm_in": 256, "tk": 512, "tb": 64, "tm_out": 512, "tm_ffn": 512, "fc": 1408}


def kernel(x, c, positions, ada_w, ada_b, w_in, mla_q_norm, mla_w_uq, mla_kv_norm, mla_w_ukv, mla_out_norm, rwkv_mu, rwkv_w0, rwkv_w2, rwkv_a0, rwkv_a2, rwkv_g2, rwkv_k_k, rwkv_k_a, rwkv_r_k, rwkv_gn_g, rwkv_gn_b, w_out, ffn_w_up, ffn_conv_w, ffn_conv_b, ffn_w_down, final_norm):
    p = dict(ada_w=ada_w, ada_b=ada_b, w_in=w_in, mla_q_norm=mla_q_norm, mla_w_uq=mla_w_uq,
             mla_kv_norm=mla_kv_norm, mla_w_ukv=mla_w_ukv, mla_out_norm=mla_out_norm,
             rwkv_mu=rwkv_mu, rwkv_w0=rwkv_w0, rwkv_w2=rwkv_w2, rwkv_a0=rwkv_a0, rwkv_a2=rwkv_a2,
             rwkv_g2=rwkv_g2, rwkv_k_k=rwkv_k_k, rwkv_k_a=rwkv_k_a, rwkv_r_k=rwkv_r_k,
             rwkv_gn_g=rwkv_gn_g, rwkv_gn_b=rwkv_gn_b, w_out=w_out, ffn_w_up=ffn_w_up,
             ffn_conv_w=ffn_conv_w, ffn_conv_b=ffn_conv_b, ffn_w_down=ffn_w_down,
             final_norm=final_norm)
    return _forward(x, c, positions, p, _CFG)
```

```python
import functools

import jax
import jax.numpy as jnp
from jax import lax
from jax.experimental import pallas as pl
from jax.experimental.pallas import tpu as pltpu

D_MODEL = 1024
DEPTH = 4
MLA_HEADS = 4
MLA_NOPE = 128
MLA_ROPE = 64
MLA_V = 128
MLA_Q_RANK = 256
MLA_KV_RANK = 128
MLA_WIDTH = MLA_HEADS * MLA_V
MLA_QK = MLA_NOPE + MLA_ROPE
ROPE_BASE = 10000.0
RWKV_WIDTH = D_MODEL - MLA_WIDTH
RWKV_HEAD = 64
RWKV_HEADS = RWKV_WIDTH // RWKV_HEAD
DECAY_LORA = 64
AAA_LORA = 64
GATE_LORA = 128
GN_EPS = 64e-5
D_FF = 2816
EPS = 1e-6

MLA_LATENT = MLA_Q_RANK + MLA_KV_RANK
MLA_IN = MLA_LATENT + MLA_ROPE
MLA_COLS = MLA_LATENT + 2 * MLA_ROPE
LORA_COLS = 2 * DECAY_LORA + 2 * AAA_LORA
RWKV_COLS = 3 * RWKV_WIDTH + GATE_LORA + 2 * DECAY_LORA + 2 * AAA_LORA
IN_COLS = MLA_COLS + RWKV_COLS
Q_HEAD_ROWS = MLA_NOPE + 2 * MLA_ROPE
BF16_ROWS = 16
V_ROWS = MLA_V + BF16_ROWS

LANES = 128
HALO = 8
VMEM_LIMIT = 56 * 1024 * 1024
LOG2E = 1.4426950408889634
DECAY_SCALE = 0.6065306597126334

F32 = jnp.float32
BF16 = jnp.bfloat16
NT_DIMS = (((1,), (1,)), ((), ()))


def _sigmoid(x):
    return 1.0 / (1.0 + jnp.exp(-x))


def _rms(x, eps=EPS):
    return x * lax.rsqrt(jnp.mean(x * x, axis=-1, keepdims=True) + eps)


def _seg_dot(x, seg):
    hi = x.astype(BF16)
    lo = (x - hi.astype(F32)).astype(BF16)
    return (jnp.dot(hi, seg, preferred_element_type=F32)
            + jnp.dot(lo, seg, preferred_element_type=F32))


def _halo_bounds(i, nt, tm):
    lo = jnp.where(i > 0, 0, HALO)
    hi = jnp.where(i < nt - 1, tm + 2 * HALO, tm + HALO)
    return lo, hi


def _adaln_kernel(c_ref, w_ref, b_ref, o_ref):
    c = c_ref[...]
    ca = c * _sigmoid(c)
    o_ref[0] = jnp.dot(ca, w_ref[0], precision=lax.Precision.HIGHEST,
                       preferred_element_type=F32) + b_ref[0]


def _adaln(c, ada_w, ada_b):
    L, D, D6 = ada_w.shape
    B = c.shape[0]
    nj = D6 // D
    return pl.pallas_call(
        _adaln_kernel,
        grid=(L, nj),
        in_specs=[
            pl.BlockSpec((B, D), lambda l, j: (0, 0)),
            pl.BlockSpec((1, D, D), lambda l, j: (l, 0, j)),
            pl.BlockSpec((1, 1, D), lambda l, j: (l, 0, j)),
        ],
        out_specs=pl.BlockSpec((1, B, D), lambda l, j: (l, 0, j)),
        out_shape=jax.ShapeDtypeStruct((L, B, D6), F32),
        compiler_params=pltpu.CompilerParams(
            dimension_semantics=("arbitrary", "arbitrary"), vmem_limit_bytes=VMEM_LIMIT),
        name="adaln",
    )(c, ada_w, ada_b.reshape(L, 1, D6))


def _rope_kernel(pos_ref, freq_ref, tab_t_ref, tab_ref):
    ang = freq_ref[...] * pos_ref[0].astype(F32)
    row = lax.broadcasted_iota(jnp.int32, ang.shape, 0)
    cos = jnp.cos(ang)
    sin = jnp.sin(ang)
    half = MLA_ROPE // 2
    tab_t = jnp.where(row < MLA_ROPE, cos, jnp.where(row < MLA_ROPE + half, -sin, sin))
    tab_t_ref[0] = tab_t
    tab_ref[0] = tab_t.T


def _rope_tables(positions, ts):
    B, S = positions.shape
    half = MLA_ROPE // 2
    inv_freq = ROPE_BASE ** (-jnp.arange(half, dtype=F32) / half)
    freq = jnp.tile(inv_freq, LANES // half).reshape(LANES, 1)
    return pl.pallas_call(
        _rope_kernel,
        grid=(B, S // ts),
        in_specs=[
            pl.BlockSpec((1, 1, ts), lambda b, i: (b, 0, i)),
            pl.BlockSpec((LANES, 1), lambda b, i: (0, 0)),
        ],
        out_specs=(pl.BlockSpec((1, LANES, ts), lambda b, i: (b, 0, i)),
                   pl.BlockSpec((1, ts, LANES), lambda b, i: (b, i, 0))),
        out_shape=(jax.ShapeDtypeStruct((B, LANES, S), F32),
                   jax.ShapeDtypeStruct((B, S, LANES), F32)),
        compiler_params=pltpu.CompilerParams(
            dimension_semantics=("arbitrary", "arbitrary"), vmem_limit_bytes=VMEM_LIMIT),
        name="rope_tables",
    )(positions.reshape(B, 1, S), freq)


def _inproj_kernel(xm_ref, xp_ref, xn_ref, mod_ref, rope_ref, rope_t_ref, w_in_ref, mu_ref,
                   qn_ref, wuq_t_ref, kvn_ref, wuk_ref, wuv_t_ref,
                   w0_ref, w2_ref, a0_ref, a2_ref, g2_ref, kk_ref, ka_ref, rk_ref, seg_ref,
                   qt_out, k_out, vt_out, r_out, vv_out, nkk_out, wf_out, wb_out,
                   kdf_out, kdb_out, bf_out, bb_out, bonus_out, g_out, *, tm, nt):
    i = pl.program_id(1)
    D = D_MODEL
    W = RWKV_WIDTH
    te = tm + 2 * HALO
    x_ext = jnp.concatenate([xp_ref[0], xm_ref[0], xn_ref[0]], axis=0)
    mod = mod_ref[0]
    sh1 = mod[:, 0:D]
    sc1 = mod[:, D:2 * D]
    h = _rms(x_ext) * (1.0 + sc1) + sh1
    z = jnp.dot(h.astype(BF16), w_in_ref[...], preferred_element_type=F32)

    zr = z[:, MLA_COLS:]
    row = lax.broadcasted_iota(jnp.int32, (te, 1), 0)
    lo, hi = _halo_bounds(i, nt, tm)
    zr = jnp.where((row >= lo) & (row < hi), zr, 0.0)
    zc = zr[HALO:HALO + tm]
    zp = pltpu.roll(zr, 1, 0)[HALO:HALO + tm]
    zn = pltpu.roll(zr, te - 1, 0)[HALO:HALO + tm]
    mu = mu_ref[...]
    zs = zc + mu[0:1] * (zp - zc) + mu[1:2] * (zn - zc)

    zm = z[HALO:HALO + tm, 0:MLA_COLS]
    hq = (_rms(zm[:, 0:MLA_Q_RANK]) * qn_ref[...]).astype(BF16)
    q_t = lax.dot_general(wuq_t_ref[...], hq, NT_DIMS, preferred_element_type=F32)
    hkv = (_rms(zm[:, MLA_Q_RANK:MLA_Q_RANK + MLA_KV_RANK]) * kvn_ref[...]).astype(BF16)
    kn = jnp.dot(hkv, wuk_ref[...], preferred_element_type=F32)
    v_t = lax.dot_general(wuv_t_ref[...], hkv, NT_DIMS, preferred_element_type=F32)
    kprod = zm[:, MLA_LATENT:MLA_COLS] * rope_ref[0]
    kpe = (kprod + pltpu.roll(kprod, MLA_ROPE, 1))[:, 0:MLA_ROPE].astype(BF16)
    rope_t = rope_t_ref[0]
    qscale = (MLA_QK ** -0.5) * LOG2E
    for hd in range(MLA_HEADS):
        qh = q_t[hd * Q_HEAD_ROWS:(hd + 1) * Q_HEAD_ROWS]
        qprod = qh[MLA_NOPE:] * rope_t
        qpe = qprod[0:MLA_ROPE] + qprod[MLA_ROPE:]
        qt_out[0, hd, 0:MLA_NOPE, :] = (qh[0:MLA_NOPE] * qscale).astype(BF16)
        qt_out[0, hd, MLA_NOPE:MLA_QK, :] = (qpe * qscale).astype(BF16)
        k_out[0, hd, :, 0:MLA_NOPE] = kn[:, hd * MLA_NOPE:(hd + 1) * MLA_NOPE].astype(BF16)
        k_out[0, hd, :, MLA_NOPE:MLA_QK] = kpe
        vt_out[0, hd, 0:MLA_V, :] = v_t[hd * MLA_V:(hd + 1) * MLA_V].astype(BF16)
        vt_out[0, hd, MLA_V:V_ROWS, :] = jnp.ones((BF16_ROWS, tm), BF16)

    r = zs[:, 0:W]
    k = zs[:, W:2 * W]
    v = zs[:, 2 * W:3 * W]
    xg = zs[:, 3 * W:3 * W + GATE_LORA]
    xw = zs[:, 3 * W + GATE_LORA:3 * W + GATE_LORA + 2 * DECAY_LORA]
    xa = zs[:, 3 * W + GATE_LORA + 2 * DECAY_LORA:]
    seg = seg_ref[...]
    g_out[0] = jnp.dot(_sigmoid(xg).astype(BF16), g2_ref[...], preferred_element_type=F32)
    kk = k * kk_ref[...]
    kkn = kk * lax.rsqrt(_seg_dot(kk * kk, seg) + 1e-12)
    lw = jnp.dot(jnp.tanh(xw).astype(BF16), w2_ref[...], preferred_element_type=F32) + w0_ref[...]
    w = jnp.exp(_sigmoid(lw) * (-DECAY_SCALE))
    a = _sigmoid(jnp.dot(xa.astype(BF16), a2_ref[...], preferred_element_type=F32) + a0_ref[...])
    ka = ka_ref[...]
    kd_f = k * (1.0 + (a[:, 0:W] - 1.0) * ka)
    kd_b = k * (1.0 + (a[:, W:] - 1.0) * ka)
    r_out[0] = r
    vv_out[0] = v
    nkk_out[0] = -kkn
    wf_out[0] = w[:, 0:W]
    wb_out[0] = w[:, W:]
    kdf_out[0] = kd_f
    kdb_out[0] = kd_b
    bf_out[0] = kkn * a[:, 0:W]
    bb_out[0] = kkn * a[:, W:]
    bonus_out[0] = _seg_dot(r * (0.5 * (kd_f + kd_b)) * rk_ref[...], seg) * v


def _inproj(x, mod_l, rope, rope_t, lw, tm):
    B, S, D = x.shape
    nt = S // tm
    hb = tm // HALO
    nhb = S // HALO
    W = RWKV_WIDTH
    H = MLA_HEADS

    def full(a):
        return pl.BlockSpec(a.shape, lambda b, i: (0,) * a.ndim)

    weights = [lw["w_in"], lw["mu"], lw["q_norm"], lw["w_uq_t"], lw["kv_norm"], lw["w_uk"],
               lw["w_uv_t"], lw["w0"], lw["w2"], lw["a0"], lw["a2"], lw["g2"], lw["k_k"],
               lw["k_a"], lw["r_k"], lw["seg"]]
    tok = pl.BlockSpec((1, tm, W), lambda b, i: (b, i, 0))
    tok_shape = jax.ShapeDtypeStruct((B, S, W), F32)
    out_shape = (
        jax.ShapeDtypeStruct((B, H, MLA_QK, S), BF16),
        jax.ShapeDtypeStruct((B, H, S, MLA_QK), BF16),
        jax.ShapeDtypeStruct((B, H, V_ROWS, S), BF16),
    ) + (tok_shape,) * 11
    out_specs = (
        pl.BlockSpec((1, H, MLA_QK, tm), lambda b, i: (b, 0, 0, i)),
        pl.BlockSpec((1, H, tm, MLA_QK), lambda b, i: (b, 0, i, 0)),
        pl.BlockSpec((1, H, V_ROWS, tm), lambda b, i: (b, 0, 0, i)),
    ) + (tok,) * 11
    return pl.pallas_call(
        functools.partial(_inproj_kernel, tm=tm, nt=nt),
        grid=(B, nt),
        in_specs=[
            pl.BlockSpec((1, tm, D), lambda b, i: (b, i, 0)),
            pl.BlockSpec((1, HALO, D), lambda b, i: (b, jnp.maximum(i * hb - 1, 0), 0)),
            pl.BlockSpec((1, HALO, D), lambda b, i: (b, jnp.minimum((i + 1) * hb, nhb - 1), 0)),
            pl.BlockSpec((1, 1, 6 * D), lambda b, i: (b, 0, 0)),
            pl.BlockSpec((1, tm, LANES), lambda b, i: (b, i, 0)),
            pl.BlockSpec((1, LANES, tm), lambda b, i: (b, 0, i)),
        ] + [full(a) for a in weights],
        out_specs=out_specs,
        out_shape=out_shape,
        compiler_params=pltpu.CompilerParams(
            dimension_semantics=("arbitrary", "arbitrary"), vmem_limit_bytes=VMEM_LIMIT),
        name="inproj",
    )(x, x, x, mod_l, rope, rope_t, *weights)


SCAN_RING = 4
_R, _W, _KD, _V, _B, _NK = range(6)


def _mixers_kernel(rf_ref, rb_ref, wf_ref, wb_ref, kdf_ref, kdb_ref, vf_ref, vb_ref, bf_ref, bb_ref,
                   nkf_ref, nkb_ref, qt_ref, k_ref, vt_ref, yf_ref, yb_ref, o_ref, *scratch,
                   tb, tq, tk, nk):
    N = RWKV_HEAD
    HP = RWKV_HEADS // 2
    B = rf_ref.shape[0]
    G = HP * B
    slabs = scratch[0:SCAN_RING]
    y_ring = scratch[SCAN_RING:2 * SCAN_RING]
    state_ref, sa_ref, s_ref, cmax_ref, acc_ref, m_ref = scratch[2 * SCAN_RING:]
    C = sa_ref.shape[-1]
    pairs = ((rf_ref, rb_ref), (wf_ref, wb_ref), (kdf_ref, kdb_ref), (vf_ref, vb_ref),
             (bf_ref, bb_ref), (nkf_ref, nkb_ref))

    def fill(slot, j):
        for idx, (f_ref, b_ref) in enumerate(pairs):
            f_row = f_ref[:, j, :]
            b_row = b_ref[:, tb - 1 - j, :]
            rows = ([f_row[:, hp * LANES:(hp + 1) * LANES] for hp in range(HP)]
                    + [b_row[:, hp * LANES:(hp + 1) * LANES] for hp in range(HP)])
            t = jnp.concatenate(rows, axis=0).T
            slabs[slot][idx] = jnp.concatenate([t[0:N], t[N:2 * N]], axis=1)

    def scores(grp, c, slot):
        q_t = qt_ref[0, 0, :, pl.ds(pl.multiple_of(grp * tq, tq), tq)]
        start = pl.multiple_of(c * tk, tk)
        s_t = jnp.dot(k_ref[0, 0, pl.ds(start, tk), :], q_t,
                      preferred_element_type=F32)
        s_ref[slot] = s_t
        cmax_ref[slot] = jnp.max(s_t, axis=0, keepdims=True)

    def consume(c, slot):
        start = pl.multiple_of(c * tk, tk)
        s_t = s_ref[slot]
        m = m_ref[...]
        m_new = jnp.maximum(m, cmax_ref[slot])
        p_t = jnp.exp2(s_t - m_new).astype(BF16)
        alpha = jnp.exp2(m - m_new)
        acc_ref[...] = acc_ref[...] * alpha + jnp.dot(
            vt_ref[0, 0, :, pl.ds(start, tk)], p_t, preferred_element_type=F32)
        m_ref[...] = m_new

    def emit(slot, j):
        y = y_ring[slot][...]
        t = jnp.concatenate([y[:, 0:2 * G], y[:, 2 * G:C]], axis=0).T
        yf_ref[:, j, :] = jnp.concatenate([t[hp * B:(hp + 1) * B] for hp in range(HP)], axis=1)
        yb_ref[:, tb - 1 - j, :] = jnp.concatenate(
            [t[G + hp * B:G + (hp + 1) * B] for hp in range(HP)], axis=1)

    @pl.when(pl.program_id(0) == 0)
    def _():
        state_ref[...] = jnp.zeros_like(state_ref)
        for y_ref in y_ring:
            y_ref[...] = jnp.zeros_like(y_ref)

    fill(0, 0)
    fill(1, 1)
    sa0 = jnp.zeros((N, C), F32)
    for kc in range(N):
        sa0 = sa0 + state_ref[kc] * slabs[0][_NK, pl.ds(kc, 1), :]
    sa_ref[...] = sa0

    def step(grp, c, cur):
        j = grp * nk + c
        nxt = (cur + 1) % SCAN_RING
        fill((cur + 2) % SCAN_RING, jnp.minimum(j + 2, tb - 1))
        emit((cur + SCAN_RING - 1) % SCAN_RING, jnp.maximum(j - 1, 0))
        scores(grp, jnp.minimum(c + 1, nk - 1), (cur + 1) % 2)
        consume(c, cur % 2)
        for hv in range(4):
            rows = slice(hv * (N // 4), (hv + 1) * (N // 4))
            sa = sa_ref[rows, :]
            vv = slabs[cur][_V, rows, :]
            y = jnp.zeros_like(sa)
            sa_next = jnp.zeros_like(sa)
            for kc in range(N):
                row = pl.ds(kc, 1)
                s_new = (state_ref[kc, rows, :] * slabs[cur][_W, row, :]
                         + sa * slabs[cur][_B, row, :] + vv * slabs[cur][_KD, row, :])
                state_ref[kc, rows, :] = s_new
                y = y + s_new * slabs[cur][_R, row, :]
                sa_next = sa_next + s_new * slabs[nxt][_NK, row, :]
            sa_ref[rows, :] = sa_next
            y_ring[cur][rows, :] = y

    def group(grp, _):
        m_ref[...] = jnp.full(m_ref.shape, -jnp.inf, F32)
        acc_ref[...] = jnp.zeros_like(acc_ref)
        scores(grp, 0, 0)

        def body(i, _):
            for u in range(SCAN_RING):
                pl.when(i >= -u)(functools.partial(step, grp, i * SCAN_RING + u, u))
            return 0

        lax.fori_loop(0, nk // SCAN_RING, body, 0)
        acc = acc_ref[...]
        o_ref[0, pl.ds(pl.multiple_of(grp * tq, tq), tq), :] = (
            acc[0:MLA_V] / acc[MLA_V:MLA_V + 1]).T
        return 0

    lax.fori_loop(0, tb // nk, group, 0)
    emit((tb - 1) % SCAN_RING, tb - 1)


def _mixers(r, w_f, w_b, kd_f, kd_b, v, b_f, b_b, nkk, q_t, k, v_t, tb, tk):
    B, S, W = r.shape
    H = MLA_HEADS
    nb = S // tb
    nk = S // tk
    gps = tb // nk
    tq = B * H * tb // gps
    qblk = S // (gps * tq)
    assert tb % nk == 0 and nk % SCAN_RING == 0 and S % (gps * tq) == 0 and nb == B * H * qblk
    chains = 2 * RWKV_HEADS * B
    fwd = pl.BlockSpec((B, tb, W), lambda i: (0, i, 0))
    bwd = pl.BlockSpec((B, tb, W), lambda i: (0, nb - 1 - i, 0))

    def bh(i):
        return i // (H * qblk), (i // qblk) % H

    return pl.pallas_call(
        functools.partial(_mixers_kernel, tb=tb, tq=tq, tk=tk, nk=nk),
        grid=(nb,),
        in_specs=[fwd, bwd] * 6 + [
            pl.BlockSpec((1, 1, MLA_QK, gps * tq), lambda i: (*bh(i), 0, i % qblk)),
            pl.BlockSpec((1, 1, S, MLA_QK), lambda i: (*bh(i), 0, 0)),
            pl.BlockSpec((1, 1, V_ROWS, S), lambda i: (*bh(i), 0, 0)),
        ],
        out_specs=(fwd, bwd,
                   pl.BlockSpec((1, gps * tq, MLA_V), lambda i: (bh(i)[0], i % qblk, bh(i)[1]))),
        out_shape=(jax.ShapeDtypeStruct((B, S, W), F32),) * 2
        + (jax.ShapeDtypeStruct((B, S, H * MLA_V), F32),),
        scratch_shapes=[pltpu.VMEM((6, RWKV_HEAD, chains), F32)] * SCAN_RING
        + [pltpu.VMEM((RWKV_HEAD, chains), F32)] * SCAN_RING
        + [pltpu.VMEM((RWKV_HEAD, RWKV_HEAD, chains), F32),
           pltpu.VMEM((RWKV_HEAD, chains), F32),
           pltpu.VMEM((2, tk, tq), F32), pltpu.VMEM((2, 1, tq), F32),
           pltpu.VMEM((V_ROWS, tq), F32), pltpu.VMEM((1, tq), F32)],
        compiler_params=pltpu.CompilerParams(
            dimension_semantics=("arbitrary",), vmem_limit_bytes=VMEM_LIMIT),
        name="token_mixers",
    )(r, r, w_f, w_b, kd_f, kd_b, v, v, b_f, b_b, nkk, nkk, q_t, k, v_t)


def _mixout_kernel(x_ref, o_ref, yf_ref, yb_ref, bonus_ref, g_ref, mod_ref, on_ref, gng_ref,
                   gnb_ref, seg_ref, wo_ref, out_ref):
    D = D_MODEL
    gt1 = mod_ref[0][:, 2 * D:3 * D]
    seg = seg_ref[...]
    y_mla = _rms(o_ref[0]) * on_ref[...]
    y = yf_ref[0] + yb_ref[0]
    inv_n = 1.0 / RWKV_HEAD
    d = y - _seg_dot(y, seg) * inv_n
    var = _seg_dot(d * d, seg) * inv_n
    yn = d * lax.rsqrt(var + GN_EPS) * gng_ref[...] + gnb_ref[...]
    y_rwkv = (yn + bonus_ref[0]) * g_ref[0]
    proj = (jnp.dot(y_mla.astype(BF16), wo_ref[0:MLA_WIDTH, :], preferred_element_type=F32)
            + jnp.dot(y_rwkv.astype(BF16), wo_ref[MLA_WIDTH:, :], preferred_element_type=F32))
    out_ref[0] = x_ref[0] + gt1 * proj


def _mixout(x, o, y_f, y_b, bonus, g, mod_l, lw, tm):
    B, S, D = x.shape
    W = RWKV_WIDTH

    def full(a):
        return pl.BlockSpec(a.shape, lambda b, i: (0,) * a.ndim)

    weights = [lw["out_norm"], lw["gn_g"], lw["gn_b"], lw["seg"], lw["w_out"]]
    tokw = pl.BlockSpec((1, tm, W), lambda b, i: (b, i, 0))
    tokd = pl.BlockSpec((1, tm, D), lambda b, i: (b, i, 0))
    return pl.pallas_call(
        _mixout_kernel,
        grid=(B, S // tm),
        in_specs=[tokd, tokw, tokw, tokw, tokw, tokw,
                  pl.BlockSpec((1, 1, 6 * D), lambda b, i: (b, 0, 0))] + [full(a) for a in weights],
        out_specs=tokd,
        out_shape=jax.ShapeDtypeStruct((B, S, D), F32),
        compiler_params=pltpu.CompilerParams(
            dimension_semantics=("arbitrary", "arbitrary"), vmem_limit_bytes=VMEM_LIMIT),
        name="mixer_out",
    )(x, o, y_f, y_b, bonus, g, mod_l, *weights)


def _ffn_kernel(xm_ref, xp_ref, xn_ref, mod_ref, wg_ref, wv_ref, cw_ref, cb_ref, wd_ref,
                out_ref, h_ref, acc_ref, *, tm, nt, nc):
    i = pl.program_id(1)
    c = pl.program_id(2)
    D = D_MODEL
    te = tm + 2 * HALO
    mod = mod_ref[0]

    @pl.when(c == 0)
    def _():
        x_ext = jnp.concatenate([xp_ref[0], xm_ref[0], xn_ref[0]], axis=0)
        sh2 = mod[:, 3 * D:4 * D]
        sc2 = mod[:, 4 * D:5 * D]
        h_ref[...] = (_rms(x_ext) * (1.0 + sc2) + sh2).astype(BF16)
        acc_ref[...] = jnp.zeros_like(acc_ref)

    h = h_ref[...]
    gate = jnp.dot(h, wg_ref[...], preferred_element_type=F32)
    val = jnp.dot(h[HALO:HALO + tm], wv_ref[...], preferred_element_type=F32)
    row = lax.broadcasted_iota(jnp.int32, (te, 1), 0)
    lo, hi = _halo_bounds(i, nt, tm)
    gate = jnp.where((row >= lo) & (row < hi), gate, 0.0)
    gp = pltpu.roll(gate, 1, 0)[HALO:HALO + tm]
    gn = pltpu.roll(gate, te - 1, 0)[HALO:HALO + tm]
    gc = gate[HALO:HALO + tm]
    cw = cw_ref[...]
    gg = cw[0:1] * gp + cw[1:2] * gc + cw[2:3] * gn + cb_ref[...]
    act = gg * _sigmoid(gg) * val
    acc_ref[...] += jnp.dot(act.astype(BF16), wd_ref[...], preferred_element_type=F32)

    @pl.when(c == nc - 1)
    def _():
        gt2 = mod[:, 5 * D:6 * D]
        out_ref[0] = xm_ref[0] + gt2 * acc_ref[...]


def _ffn(x, mod_l, lw, tm, fc):
    B, S, D = x.shape
    nt = S // tm
    nc = D_FF // fc
    hb = tm // HALO
    nhb = S // HALO
    return pl.pallas_call(
        functools.partial(_ffn_kernel, tm=tm, nt=nt, nc=nc),
        grid=(B, nt, nc),
        in_specs=[
            pl.BlockSpec((1, tm, D), lambda b, i, c: (b, i, 0)),
            pl.BlockSpec((1, HALO, D), lambda b, i, c: (b, jnp.maximum(i * hb - 1, 0), 0)),
            pl.BlockSpec((1, HALO, D), lambda b, i, c: (b, jnp.minimum((i + 1) * hb, nhb - 1), 0)),
            pl.BlockSpec((1, 1, 6 * D), lambda b, i, c: (b, 0, 0)),
            pl.BlockSpec((D, fc), lambda b, i, c: (0, c)),
            pl.BlockSpec((D, fc), lambda b, i, c: (0, nc + c)),
            pl.BlockSpec((3, fc), lambda b, i, c: (0, c)),
            pl.BlockSpec((1, fc), lambda b, i, c: (0, c)),
            pl.BlockSpec((fc, D), lambda b, i, c: (c, 0)),
        ],
        out_specs=pl.BlockSpec((1, tm, D), lambda b, i, c: (b, i, 0)),
        out_shape=jax.ShapeDtypeStruct((B, S, D), F32),
        scratch_shapes=[pltpu.VMEM((tm + 2 * HALO, D), BF16), pltpu.VMEM((tm, D), F32)],
        compiler_params=pltpu.CompilerParams(
            dimension_semantics=("arbitrary", "arbitrary", "arbitrary"),
            vmem_limit_bytes=VMEM_LIMIT),
        name="conv_ffn",
    )(x, x, x, mod_l, lw["w_up"], lw["w_up"], lw["conv_w"], lw["conv_b"], lw["w_down"])


def _final_norm_kernel(x_ref, g_ref, o_ref):
    o_ref[0] = _rms(x_ref[0]) * g_ref[...]


def _final_norm(x, gain, tm):
    B, S, D = x.shape
    tok = pl.BlockSpec((1, tm, D), lambda b, i: (b, i, 0))
    return pl.pallas_call(
        _final_norm_kernel,
        grid=(B, S // tm),
        in_specs=[tok, pl.BlockSpec((1, D), lambda b, i: (0, 0))],
        out_specs=tok,
        out_shape=jax.ShapeDtypeStruct((B, S, D), F32),
        compiler_params=pltpu.CompilerParams(
            dimension_semantics=("arbitrary", "arbitrary"), vmem_limit_bytes=VMEM_LIMIT),
        name="final_norm",
    )(x, gain.reshape(1, D))


def _block_diag2(a, b):
    za = jnp.zeros((a.shape[0], b.shape[1]), a.dtype)
    zb = jnp.zeros((b.shape[0], a.shape[1]), a.dtype)
    return jnp.concatenate([jnp.concatenate([a, za], axis=1),
                            jnp.concatenate([zb, b], axis=1)], axis=0)


def _swap_halves(cols):
    half = cols.shape[-1] // 2
    return jnp.concatenate([cols[..., half:], cols[..., :half]], axis=-1)


def _layer_weights(l, p):
    W = RWKV_WIDTH
    w_in = p["w_in"][l]
    mla = w_in[:, :MLA_IN]
    rw = w_in[:, MLA_IN:]
    kpe = mla[:, MLA_LATENT:MLA_IN]

    def lora_last(a):
        return jnp.concatenate(
            [a[:, :3 * W], a[:, 3 * W + LORA_COLS:], a[:, 3 * W:3 * W + LORA_COLS]], axis=1)

    rw_perm = lora_last(rw)
    mu_perm = lora_last(p["rwkv_mu"][l])
    w_in_perm = jnp.concatenate([mla, _swap_halves(kpe), rw_perm], axis=1).astype(BF16)

    w_uq = p["mla_w_uq"][l].reshape(MLA_Q_RANK, MLA_HEADS, MLA_QK)
    pe = w_uq[:, :, MLA_NOPE:]
    w_uq_t = jnp.concatenate([w_uq, _swap_halves(pe)], axis=-1).reshape(
        MLA_Q_RANK, MLA_HEADS * Q_HEAD_ROWS).T.astype(BF16)
    w_ukv = p["mla_w_ukv"][l].reshape(MLA_KV_RANK, MLA_HEADS, MLA_NOPE + MLA_V)
    w_uk = w_ukv[:, :, :MLA_NOPE].reshape(MLA_KV_RANK, MLA_HEADS * MLA_NOPE).astype(BF16)
    w_uv_t = w_ukv[:, :, MLA_NOPE:].reshape(MLA_KV_RANK, MLA_HEADS * MLA_V).T.astype(BF16)

    head = jnp.arange(W) // RWKV_HEAD
    seg = (head[:, None] == head[None, :]).astype(BF16)
    return {
        "w_in": w_in_perm,
        "mu": mu_perm,
        "q_norm": p["mla_q_norm"][l].reshape(1, -1),
        "w_uq_t": w_uq_t,
        "kv_norm": p["mla_kv_norm"][l].reshape(1, -1),
        "w_uk": w_uk,
        "w_uv_t": w_uv_t,
        "out_norm": p["mla_out_norm"][l].reshape(1, -1),
        "w0": p["rwkv_w0"][l].reshape(1, 2 * W),
        "w2": _block_diag2(p["rwkv_w2"][l, 0], p["rwkv_w2"][l, 1]).astype(BF16),
        "a0": p["rwkv_a0"][l].reshape(1, 2 * W),
        "a2": _block_diag2(p["rwkv_a2"][l, 0], p["rwkv_a2"][l, 1]).astype(BF16),
        "g2": p["rwkv_g2"][l].astype(BF16),
        "k_k": p["rwkv_k_k"][l].reshape(1, W),
        "k_a": p["rwkv_k_a"][l].reshape(1, W),
        "r_k": p["rwkv_r_k"][l].reshape(1, W),
        "gn_g": p["rwkv_gn_g"][l].reshape(1, W),
        "gn_b": p["rwkv_gn_b"][l].reshape(1, W),
        "seg": seg,
        "w_out": p["w_out"][l].astype(BF16),
        "w_up": p["ffn_w_up"][l].astype(BF16),
        "conv_w": p["ffn_conv_w"][l],
        "conv_b": p["ffn_conv_b"][l].reshape(1, D_FF),
        "w_down": p["ffn_w_down"][l].astype(BF16),
    }


def _forward(x, c, positions, p, cfg):
    B, S, D = x.shape
    mod = _adaln(c, p["ada_w"], p["ada_b"])
    rope_t, rope = _rope_tables(positions, cfg["tm_in"])
    for l in range(DEPTH):
        lw = _layer_weights(l, p)
        mod_l = mod[l].reshape(B, 1, 6 * D)
        (q_t, k, v_t, r, vv, nkk, w_f, w_b, kd_f, kd_b, b_f, b_b, bonus, g) = _inproj(
            x, mod_l, rope, rope_t, lw, cfg["tm_in"])
        y_f, y_b, o = _mixers(r, w_f, w_b, kd_f, kd_b, vv, b_f, b_b, nkk, q_t, k, v_t,
                              cfg["tb"], cfg["tk"])
        x = _mixout(x, o, y_f, y_b, bonus, g, mod_l, lw, cfg["tm_out"])
        x = _ffn(x, mod_l, lw, cfg["tm_ffn"], cfg["fc"])
    return _final_norm(x, p["final_norm"], cfg["tm_out"])


_CFG = {"tm_in": 256, "tk": 512, "tb": 64, "tm_out": 512, "tm_ffn": 512, "fc": 1408}


def kernel(x, c, positions, ada_w, ada_b, w_in, mla_q_norm, mla_w_uq, mla_kv_norm, mla_w_ukv, mla_out_norm, rwkv_mu, rwkv_w0, rwkv_w2, rwkv_a0, rwkv_a2, rwkv_g2, rwkv_k_k, rwkv_k_a, rwkv_r_k, rwkv_gn_g, rwkv_gn_b, w_out, ffn_w_up, ffn_conv_w, ffn_conv_b, ffn_w_down, final_norm):
    p = dict(ada_w=ada_w, ada_b=ada_b, w_in=w_in, mla_q_norm=mla_q_norm, mla_w_uq=mla_w_uq,
             mla_kv_norm=mla_kv_norm, mla_w_ukv=mla_w_ukv, mla_out_norm=mla_out_norm,
             rwkv_mu=rwkv_mu, rwkv_w0=rwkv_w0, rwkv_w2=rwkv_w2, rwkv_a0=rwkv_a0, rwkv_a2=rwkv_a2,
             rwkv_g2=rwkv_g2, rwkv_k_k=rwkv_k_k, rwkv_k_a=rwkv_k_a, rwkv_r_k=rwkv_r_k,
             rwkv_gn_g=rwkv_gn_g, rwkv_gn_b=rwkv_gn_b, w_out=w_out, ffn_w_up=ffn_w_up,
             ffn_conv_w=ffn_conv_w, ffn_conv_b=ffn_conv_b, ffn_w_down=ffn_w_down,
             final_norm=final_norm)
    return _forward(x, c, positions, p, _CFG)
```

```python
import functools

import jax
import jax.numpy as jnp
from jax import lax
from jax.experimental import pallas as pl
from jax.experimental.pallas import tpu as pltpu

D_MODEL = 1024
DEPTH = 4
MLA_HEADS = 4
MLA_NOPE = 128
MLA_ROPE = 64
MLA_V = 128
MLA_Q_RANK = 256
MLA_KV_RANK = 128
MLA_WIDTH = MLA_HEADS * MLA_V
MLA_QK = MLA_NOPE + MLA_ROPE
ROPE_BASE = 10000.0
RWKV_WIDTH = D_MODEL - MLA_WIDTH
RWKV_HEAD = 64
RWKV_HEADS = RWKV_WIDTH // RWKV_HEAD
DECAY_LORA = 64
AAA_LORA = 64
GATE_LORA = 128
GN_EPS = 64e-5
D_FF = 2816
EPS = 1e-6

MLA_LATENT = MLA_Q_RANK + MLA_KV_RANK
MLA_IN = MLA_LATENT + MLA_ROPE
MLA_COLS = MLA_LATENT + 2 * MLA_ROPE
LORA_COLS = 2 * DECAY_LORA + 2 * AAA_LORA
RWKV_COLS = 3 * RWKV_WIDTH + GATE_LORA + 2 * DECAY_LORA + 2 * AAA_LORA
IN_COLS = MLA_COLS + RWKV_COLS
Q_HEAD_ROWS = MLA_NOPE + 2 * MLA_ROPE
BF16_ROWS = 16
V_ROWS = MLA_V + BF16_ROWS

LANES = 128
HALO = 8
VMEM_LIMIT = 56 * 1024 * 1024
LOG2E = 1.4426950408889634
DECAY_SCALE = 0.6065306597126334

F32 = jnp.float32
BF16 = jnp.bfloat16
NT_DIMS = (((1,), (1,)), ((), ()))


def _sigmoid(x):
    return 1.0 / (1.0 + jnp.exp(-x))


def _rms(x, eps=EPS):
    return x * lax.rsqrt(jnp.mean(x * x, axis=-1, keepdims=True) + eps)


def _seg_dot(x, seg):
    hi = x.astype(BF16)
    lo = (x - hi.astype(F32)).astype(BF16)
    return (jnp.dot(hi, seg, preferred_element_type=F32)
            + jnp.dot(lo, seg, preferred_element_type=F32))


def _halo_bounds(i, nt, tm):
    lo = jnp.where(i > 0, 0, HALO)
    hi = jnp.where(i < nt - 1, tm + 2 * HALO, tm + HALO)
    return lo, hi


def _adaln_kernel(c_ref, w_ref, b_ref, o_ref):
    c = c_ref[...]
    ca = c * _sigmoid(c)
    o_ref[0] = jnp.dot(ca, w_ref[0], precision=lax.Precision.HIGHEST,
                       preferred_element_type=F32) + b_ref[0]


def _adaln(c, ada_w, ada_b):
    L, D, D6 = ada_w.shape
    B = c.shape[0]
    nj = D6 // D
    return pl.pallas_call(
        _adaln_kernel,
        grid=(L, nj),
        in_specs=[
            pl.BlockSpec((B, D), lambda l, j: (0, 0)),
            pl.BlockSpec((1, D, D), lambda l, j: (l, 0, j)),
            pl.BlockSpec((1, 1, D), lambda l, j: (l, 0, j)),
        ],
        out_specs=pl.BlockSpec((1, B, D), lambda l, j: (l, 0, j)),
        out_shape=jax.ShapeDtypeStruct((L, B, D6), F32),
        compiler_params=pltpu.CompilerParams(
            dimension_semantics=("arbitrary", "arbitrary"), vmem_limit_bytes=VMEM_LIMIT),
        name="adaln",
    )(c, ada_w, ada_b.reshape(L, 1, D6))


def _rope_kernel(pos_ref, freq_ref, tab_t_ref, tab_ref):
    ang = freq_ref[...] * pos_ref[0].astype(F32)
    row = lax.broadcasted_iota(jnp.int32, ang.shape, 0)
    cos = jnp.cos(ang)
    sin = jnp.sin(ang)
    half = MLA_ROPE // 2
    tab_t = jnp.where(row < MLA_ROPE, cos, jnp.where(row < MLA_ROPE + half, -sin, sin))
    tab_t_ref[0] = tab_t
    tab_ref[0] = tab_t.T


def _rope_tables(positions, ts):
    B, S = positions.shape
    half = MLA_ROPE // 2
    inv_freq = ROPE_BASE ** (-jnp.arange(half, dtype=F32) / half)
    freq = jnp.tile(inv_freq, LANES // half).reshape(LANES, 1)
    return pl.pallas_call(
        _rope_kernel,
        grid=(B, S // ts),
        in_specs=[
            pl.BlockSpec((1, 1, ts), lambda b, i: (b, 0, i)),
            pl.BlockSpec((LANES, 1), lambda b, i: (0, 0)),
        ],
        out_specs=(pl.BlockSpec((1, LANES, ts), lambda b, i: (b, 0, i)),
                   pl.BlockSpec((1, ts, LANES), lambda b, i: (b, i, 0))),
        out_shape=(jax.ShapeDtypeStruct((B, LANES, S), F32),
                   jax.ShapeDtypeStruct((B, S, LANES), F32)),
        compiler_params=pltpu.CompilerParams(
            dimension_semantics=("arbitrary", "arbitrary"), vmem_limit_bytes=VMEM_LIMIT),
        name="rope_tables",
    )(positions.reshape(B, 1, S), freq)


def _inproj_kernel(xm_ref, xp_ref, xn_ref, mod_ref, rope_ref, rope_t_ref, w_in_ref, mu_ref,
                   qn_ref, wuq_t_ref, kvn_ref, wuk_ref, wuv_t_ref,
                   w0_ref, w2_ref, a0_ref, a2_ref, g2_ref, kk_ref, ka_ref, rk_ref, seg_ref,
                   qt_out, k_out, vt_out, r_out, vv_out, nkk_out, wf_out, wb_out,
                   kdf_out, kdb_out, bf_out, bb_out, bonus_out, g_out, *, tm, nt):
    i = pl.program_id(1)
    D = D_MODEL
    W = RWKV_WIDTH
    te = tm + 2 * HALO
    x_ext = jnp.concatenate([xp_ref[0], xm_ref[0], xn_ref[0]], axis=0)
    mod = mod_ref[0]
    sh1 = mod[:, 0:D]
    sc1 = mod[:, D:2 * D]
    h = _rms(x_ext) * (1.0 + sc1) + sh1
    z = jnp.dot(h.astype(BF16), w_in_ref[...], preferred_element_type=F32)

    zr = z[:, MLA_COLS:]
    row = lax.broadcasted_iota(jnp.int32, (te, 1), 0)
    lo, hi = _halo_bounds(i, nt, tm)
    zr = jnp.where((row >= lo) & (row < hi), zr, 0.0)
    zc = zr[HALO:HALO + tm]
    zp = pltpu.roll(zr, 1, 0)[HALO:HALO + tm]
    zn = pltpu.roll(zr, te - 1, 0)[HALO:HALO + tm]
    mu = mu_ref[...]
    zs = zc + mu[0:1] * (zp - zc) + mu[1:2] * (zn - zc)

    zm = z[HALO:HALO + tm, 0:MLA_COLS]
    hq = (_rms(zm[:, 0:MLA_Q_RANK]) * qn_ref[...]).astype(BF16)
    q_t = lax.dot_general(wuq_t_ref[...], hq, NT_DIMS, preferred_element_type=F32)
    hkv = (_rms(zm[:, MLA_Q_RANK:MLA_Q_RANK + MLA_KV_RANK]) * kvn_ref[...]).astype(BF16)
    kn = jnp.dot(hkv, wuk_ref[...], preferred_element_type=F32)
    v_t = lax.dot_general(wuv_t_ref[...], hkv, NT_DIMS, preferred_element_type=F32)
    kprod = zm[:, MLA_LATENT:MLA_COLS] * rope_ref[0]
    kpe = (kprod + pltpu.roll(kprod, MLA_ROPE, 1))[:, 0:MLA_ROPE].astype(BF16)
    rope_t = rope_t_ref[0]
    qscale = (MLA_QK ** -0.5) * LOG2E
    for hd in range(MLA_HEADS):
        qh = q_t[hd * Q_HEAD_ROWS:(hd + 1) * Q_HEAD_ROWS]
        qprod = qh[MLA_NOPE:] * rope_t
        qpe = qprod[0:MLA_ROPE] + qprod[MLA_ROPE:]
        qt_out[0, hd, 0:MLA_NOPE, :] = (qh[0:MLA_NOPE] * qscale).astype(BF16)
        qt_out[0, hd, MLA_NOPE:MLA_QK, :] = (qpe * qscale).astype(BF16)
        k_out[0, hd, :, 0:MLA_NOPE] = kn[:, hd * MLA_NOPE:(hd + 1) * MLA_NOPE].astype(BF16)
        k_out[0, hd, :, MLA_NOPE:MLA_QK] = kpe
        vt_out[0, hd, 0:MLA_V, :] = v_t[hd * MLA_V:(hd + 1) * MLA_V].astype(BF16)
        vt_out[0, hd, MLA_V:V_ROWS, :] = jnp.ones((BF16_ROWS, tm), BF16)

    r = zs[:, 0:W]
    k = zs[:, W:2 * W]
    v = zs[:, 2 * W:3 * W]
    xg = zs[:, 3 * W:3 * W + GATE_LORA]
    xw = zs[:, 3 * W + GATE_LORA:3 * W + GATE_LORA + 2 * DECAY_LORA]
    xa = zs[:, 3 * W + GATE_LORA + 2 * DECAY_LORA:]
    seg = seg_ref[...]
    g_out[0] = jnp.dot(_sigmoid(xg).astype(BF16), g2_ref[...], preferred_element_type=F32)
    kk = k * kk_ref[...]
    kkn = kk * lax.rsqrt(_seg_dot(kk * kk, seg) + 1e-12)
    lw = jnp.dot(jnp.tanh(xw).astype(BF16), w2_ref[...], preferred_element_type=F32) + w0_ref[...]
    w = jnp.exp(_sigmoid(lw) * (-DECAY_SCALE))
    a = _sigmoid(jnp.dot(xa.astype(BF16), a2_ref[...], preferred_element_type=F32) + a0_ref[...])
    ka = ka_ref[...]
    kd_f = k * (1.0 + (a[:, 0:W] - 1.0) * ka)
    kd_b = k * (1.0 + (a[:, W:] - 1.0) * ka)
    r_out[0] = r
    vv_out[0] = v
    nkk_out[0] = -kkn
    wf_out[0] = w[:, 0:W]
    wb_out[0] = w[:, W:]
    kdf_out[0] = kd_f
    kdb_out[0] = kd_b
    bf_out[0] = kkn * a[:, 0:W]
    bb_out[0] = kkn * a[:, W:]
    bonus_out[0] = _seg_dot(r * (0.5 * (kd_f + kd_b)) * rk_ref[...], seg) * v


def _inproj(x, mod_l, rope, rope_t, lw, tm):
    B, S, D = x.shape
    nt = S // tm
    hb = tm // HALO
    nhb = S // HALO
    W = RWKV_WIDTH
    H = MLA_HEADS

    def full(a):
        return pl.BlockSpec(a.shape, lambda b, i: (0,) * a.ndim)

    weights = [lw["w_in"], lw["mu"], lw["q_norm"], lw["w_uq_t"], lw["kv_norm"], lw["w_uk"],
               lw["w_uv_t"], lw["w0"], lw["w2"], lw["a0"], lw["a2"], lw["g2"], lw["k_k"],
               lw["k_a"], lw["r_k"], lw["seg"]]
    tok = pl.BlockSpec((1, tm, W), lambda b, i: (b, i, 0))
    tok_shape = jax.ShapeDtypeStruct((B, S, W), F32)
    out_shape = (
        jax.ShapeDtypeStruct((B, H, MLA_QK, S), BF16),
        jax.ShapeDtypeStruct((B, H, S, MLA_QK), BF16),
        jax.ShapeDtypeStruct((B, H, V_ROWS, S), BF16),
    ) + (tok_shape,) * 11
    out_specs = (
        pl.BlockSpec((1, H, MLA_QK, tm), lambda b, i: (b, 0, 0, i)),
        pl.BlockSpec((1, H, tm, MLA_QK), lambda b, i: (b, 0, i, 0)),
        pl.BlockSpec((1, H, V_ROWS, tm), lambda b, i: (b, 0, 0, i)),
    ) + (tok,) * 11
    return pl.pallas_call(
        functools.partial(_inproj_kernel, tm=tm, nt=nt),
        grid=(B, nt),
        in_specs=[
            pl.BlockSpec((1, tm, D), lambda b, i: (b, i, 0)),
            pl.BlockSpec((1, HALO, D), lambda b, i: (b, jnp.maximum(i * hb - 1, 0), 0)),
            pl.BlockSpec((1, HALO, D), lambda b, i: (b, jnp.minimum((i + 1) * hb, nhb - 1), 0)),
            pl.BlockSpec((1, 1, 6 * D), lambda b, i: (b, 0, 0)),
            pl.BlockSpec((1, tm, LANES), lambda b, i: (b, i, 0)),
            pl.BlockSpec((1, LANES, tm), lambda b, i: (b, 0, i)),
        ] + [full(a) for a in weights],
        out_specs=out_specs,
        out_shape=out_shape,
        compiler_params=pltpu.CompilerParams(
            dimension_semantics=("arbitrary", "arbitrary"), vmem_limit_bytes=VMEM_LIMIT),
        name="inproj",
    )(x, x, x, mod_l, rope, rope_t, *weights)


SCAN_RING = 4
_R, _W, _KD, _V, _B, _NK = range(6)


def _mixers_kernel(rf_ref, rb_ref, wf_ref, wb_ref, kdf_ref, kdb_ref, vf_ref, vb_ref, bf_ref, bb_ref,
                   nkf_ref, nkb_ref, qt_ref, k_ref, vt_ref, yf_ref, yb_ref, o_ref, *scratch,
                   tb, tq, tk, nk):
    N = RWKV_HEAD
    HP = RWKV_HEADS // 2
    B = rf_ref.shape[0]
    G = HP * B
    slabs = scratch[0:SCAN_RING]
    y_ring = scratch[SCAN_RING:2 * SCAN_RING]
    state_ref, sa_ref, s_ref, cmax_ref, acc_ref, m_ref = scratch[2 * SCAN_RING:]
    C = sa_ref.shape[-1]
    pairs = ((rf_ref, rb_ref), (wf_ref, wb_ref), (kdf_ref, kdb_ref), (vf_ref, vb_ref),
             (bf_ref, bb_ref), (nkf_ref, nkb_ref))

    def fill(slot, j):
        for idx, (f_ref, b_ref) in enumerate(pairs):
            f_row = f_ref[:, j, :]
            b_row = b_ref[:, tb - 1 - j, :]
            rows = ([f_row[:, hp * LANES:(hp + 1) * LANES] for hp in range(HP)]
                    + [b_row[:, hp * LANES:(hp + 1) * LANES] for hp in range(HP)])
            t = jnp.concatenate(rows, axis=0).T
            slabs[slot][idx] = jnp.concatenate([t[0:N], t[N:2 * N]], axis=1)

    def scores(grp, c, slot):
        q_t = qt_ref[0, 0, :, pl.ds(pl.multiple_of(grp * tq, tq), tq)]
        start = pl.multiple_of(c * tk, tk)
        s_t = jnp.dot(k_ref[0, 0, pl.ds(start, tk), :], q_t,
                      preferred_element_type=F32)
        s_ref[slot] = s_t
        cmax_ref[slot] = jnp.max(s_t, axis=0, keepdims=True)

    def consume(c, slot):
        start = pl.multiple_of(c * tk, tk)
        s_t = s_ref[slot]
        m = m_ref[...]
        m_new = jnp.maximum(m, cmax_ref[slot])
        p_t = jnp.exp2(s_t - m_new).astype(BF16)
        alpha = jnp.exp2(m - m_new)
        acc_ref[...] = acc_ref[...] * alpha + jnp.dot(
            vt_ref[0, 0, :, pl.ds(start, tk)], p_t, preferred_element_type=F32)
        m_ref[...] = m_new

    def emit(slot, j):
        y = y_ring[slot][...]
        t = jnp.concatenate([y[:, 0:2 * G], y[:, 2 * G:C]], axis=0).T
        yf_ref[:, j, :] = jnp.concatenate([t[hp * B:(hp + 1) * B] for hp in range(HP)], axis=1)
        yb_ref[:, tb - 1 - j, :] = jnp.concatenate(
            [t[G + hp * B:G + (hp + 1) * B] for hp in range(HP)], axis=1)

    @pl.when(pl.program_id(0) == 0)
    def _():
        state_ref[...] = jnp.zeros_like(state_ref)
        for y_ref in y_ring:
            y_ref[...] = jnp.zeros_like(y_ref)

    fill(0, 0)
    fill(1, 1)
    sa0 = jnp.zeros((N, C), F32)
    for kc in range(N):
        sa0 = sa0 + state_ref[kc] * slabs[0][_NK, pl.ds(kc, 1), :]
    sa_ref[...] = sa0

    def step(grp, c, cur):
        j = grp * nk + c
        nxt = (cur + 1) % SCAN_RING
        fill((cur + 2) % SCAN_RING, jnp.minimum(j + 2, tb - 1))
        emit((cur + SCAN_RING - 1) % SCAN_RING, jnp.maximum(j - 1, 0))
        scores(grp, jnp.minimum(c + 1, nk - 1), (cur + 1) % 2)
        consume(c, cur % 2)
        for hv in range(4):
            rows = slice(hv * (N // 4), (hv + 1) * (N // 4))
            sa = sa_ref[rows, :]
            vv = slabs[cur][_V, rows, :]
            y = jnp.zeros_like(sa)
            sa_next = jnp.zeros_like(sa)
            for kc in range(N):
                row = pl.ds(kc, 1)
                s_new = (state_ref[kc, rows, :] * slabs[cur][_W, row, :]
                         + sa * slabs[cur][_B, row, :] + vv * slabs[cur][_KD, row, :])
                state_ref[kc, rows, :] = s_new
                y = y + s_new * slabs[cur][_R, row, :]
                sa_next = sa_next + s_new * slabs[nxt][_NK, row, :]
            sa_ref[rows, :] = sa_next
            y_ring[cur][rows, :] = y

    def group(grp, _):
        m_ref[...] = jnp.full(m_ref.shape, -jnp.inf, F32)
        acc_ref[...] = jnp.zeros_like(acc_ref)
        scores(grp, 0, 0)

        def body(i, _):
            for u in range(SCAN_RING):
                pl.when(i >= -u)(functools.partial(step, grp, i * SCAN_RING + u, u))
            return 0

        lax.fori_loop(0, nk // SCAN_RING, body, 0)
        acc = acc_ref[...]
        o_ref[0, pl.ds(pl.multiple_of(grp * tq, tq), tq), :] = (
            acc[0:MLA_V] / acc[MLA_V:MLA_V + 1]).T
        return 0

    lax.fori_loop(0, tb // nk, group, 0)
    emit((tb - 1) % SCAN_RING, tb - 1)


def _mixers(r, w_f, w_b, kd_f, kd_b, v, b_f, b_b, nkk, q_t, k, v_t, tb, tk):
    B, S, W = r.shape
    H = MLA_HEADS
    nb = S // tb
    nk = S // tk
    gps = tb // nk
    tq = B * H * tb // gps
    qblk = S // (gps * tq)
    assert tb % nk == 0 and nk % SCAN_RING == 0 and S % (gps * tq) == 0 and nb == B * H * qblk
    chains = 2 * RWKV_HEADS * B
    fwd = pl.BlockSpec((B, tb, W), lambda i: (0, i, 0))
    bwd = pl.BlockSpec((B, tb, W), lambda i: (0, nb - 1 - i, 0))

    def bh(i):
        return i // (H * qblk), (i // qblk) % H

    return pl.pallas_call(
        functools.partial(_mixers_kernel, tb=tb, tq=tq, tk=tk, nk=nk),
        grid=(nb,),
        in_specs=[fwd, bwd] * 6 + [
            pl.BlockSpec((1, 1, MLA_QK, gps * tq), lambda i: (*bh(i), 0, i % qblk)),
            pl.BlockSpec((1, 1, S, MLA_QK), lambda i: (*bh(i), 0, 0)),
            pl.BlockSpec((1, 1, V_ROWS, S), lambda i: (*bh(i), 0, 0)),
        ],
        out_specs=(fwd, bwd,
                   pl.BlockSpec((1, gps * tq, MLA_V), lambda i: (bh(i)[0], i % qblk, bh(i)[1]))),
        out_shape=(jax.ShapeDtypeStruct((B, S, W), F32),) * 2
        + (jax.ShapeDtypeStruct((B, S, H * MLA_V), F32),),
        scratch_shapes=[pltpu.VMEM((6, RWKV_HEAD, chains), F32)] * SCAN_RING
        + [pltpu.VMEM((RWKV_HEAD, chains), F32)] * SCAN_RING
        + [pltpu.VMEM((RWKV_HEAD, RWKV_HEAD, chains), F32),
           pltpu.VMEM((RWKV_HEAD, chains), F32),
           pltpu.VMEM((2, tk, tq), F32), pltpu.VMEM((2, 1, tq), F32),
           pltpu.VMEM((V_ROWS, tq), F32), pltpu.VMEM((1, tq), F32)],
        compiler_params=pltpu.CompilerParams(
            dimension_semantics=("arbitrary",), vmem_limit_bytes=VMEM_LIMIT),
        name="token_mixers",
    )(r, r, w_f, w_b, kd_f, kd_b, v, v, b_f, b_b, nkk, nkk, q_t, k, v_t)


def _mixout_kernel(x_ref, o_ref, yf_ref, yb_ref, bonus_ref, g_ref, mod_ref, on_ref, gng_ref,
                   gnb_ref, seg_ref, wo_ref, out_ref):
    D = D_MODEL
    gt1 = mod_ref[0][:, 2 * D:3 * D]
    seg = seg_ref[...]
    y_mla = _rms(o_ref[0]) * on_ref[...]
    y = yf_ref[0] + yb_ref[0]
    inv_n = 1.0 / RWKV_HEAD
    d = y - _seg_dot(y, seg) * inv_n
    var = _seg_dot(d * d, seg) * inv_n
    yn = d * lax.rsqrt(var + GN_EPS) * gng_ref[...] + gnb_ref[...]
    y_rwkv = (yn + bonus_ref[0]) * g_ref[0]
    proj = (jnp.dot(y_mla.astype(BF16), wo_ref[0:MLA_WIDTH, :], preferred_element_type=F32)
            + jnp.dot(y_rwkv.astype(BF16), wo_ref[MLA_WIDTH:, :], preferred_element_type=F32))
    out_ref[0] = x_ref[0] + gt1 * proj


def _mixout(x, o, y_f, y_b, bonus, g, mod_l, lw, tm):
    B, S, D = x.shape
    W = RWKV_WIDTH

    def full(a):
        return pl.BlockSpec(a.shape, lambda b, i: (0,) * a.ndim)

    weights = [lw["out_norm"], lw["gn_g"], lw["gn_b"], lw["seg"], lw["w_out"]]
    tokw = pl.BlockSpec((1, tm, W), lambda b, i: (b, i, 0))
    tokd = pl.BlockSpec((1, tm, D), lambda b, i: (b, i, 0))
    return pl.pallas_call(
        _mixout_kernel,
        grid=(B, S // tm),
        in_specs=[tokd, tokw, tokw, tokw, tokw, tokw,
                  pl.BlockSpec((1, 1, 6 * D), lambda b, i: (b, 0, 0))] + [full(a) for a in weights],
        out_specs=tokd,
        out_shape=jax.ShapeDtypeStruct((B, S, D), F32),
        compiler_params=pltpu.CompilerParams(
            dimension_semantics=("arbitrary", "arbitrary"), vmem_limit_bytes=VMEM_LIMIT),
        name="mixer_out",
    )(x, o, y_f, y_b, bonus, g, mod_l, *weights)


def _ffn_kernel(xm_ref, xp_ref, xn_ref, mod_ref, wup_ref, cw_ref, cb_ref, wd_ref, out_ref, *,
                tm, nt, fc):
    i = pl.program_id(1)
    D = D_MODEL
    te = tm + 2 * HALO
    mod = mod_ref[0]
    x_ext = jnp.concatenate([xp_ref[0], xm_ref[0], xn_ref[0]], axis=0)
    sh2 = mod[:, 3 * D:4 * D]
    sc2 = mod[:, 4 * D:5 * D]
    h = (_rms(x_ext) * (1.0 + sc2) + sh2).astype(BF16)
    hm = h[HALO:HALO + tm]
    row = lax.broadcasted_iota(jnp.int32, (te, 1), 0)
    lo, hi = _halo_bounds(i, nt, tm)
    inside = (row >= lo) & (row < hi)
    acc = jnp.zeros((tm, D), F32)
    for s in range(D_FF // fc):
        cols = slice(s * fc, (s + 1) * fc)
        gate = jnp.dot(h, wup_ref[:, cols], preferred_element_type=F32)
        val = jnp.dot(hm, wup_ref[:, D_FF + s * fc:D_FF + (s + 1) * fc],
                      preferred_element_type=F32)
        gate = jnp.where(inside, gate, 0.0)
        gp = pltpu.roll(gate, 1, 0)[HALO:HALO + tm]
        gn = pltpu.roll(gate, te - 1, 0)[HALO:HALO + tm]
        gc = gate[HALO:HALO + tm]
        gg = (cw_ref[0:1, cols] * gp + cw_ref[1:2, cols] * gc + cw_ref[2:3, cols] * gn
              + cb_ref[:, cols])
        act = gg * _sigmoid(gg) * val
        acc = acc + jnp.dot(act.astype(BF16), wd_ref[cols, :], preferred_element_type=F32)
    gt2 = mod[:, 5 * D:6 * D]
    out_ref[0] = xm_ref[0] + gt2 * acc


def _ffn(x, mod_l, lw, tm, fc):
    B, S, D = x.shape
    nt = S // tm
    hb = tm // HALO
    nhb = S // HALO

    def resident(a):
        return pl.BlockSpec(a.shape, lambda b, i: (0,) * a.ndim, pipeline_mode=pl.Buffered(1))

    weights = [lw["w_up"], lw["conv_w"], lw["conv_b"], lw["w_down"]]
    return pl.pallas_call(
        functools.partial(_ffn_kernel, tm=tm, nt=nt, fc=fc),
        grid=(B, nt),
        in_specs=[
            pl.BlockSpec((1, tm, D), lambda b, i: (b, i, 0)),
            pl.BlockSpec((1, HALO, D), lambda b, i: (b, jnp.maximum(i * hb - 1, 0), 0)),
            pl.BlockSpec((1, HALO, D), lambda b, i: (b, jnp.minimum((i + 1) * hb, nhb - 1), 0)),
            pl.BlockSpec((1, 1, 6 * D), lambda b, i: (b, 0, 0)),
        ] + [resident(a) for a in weights],
        out_specs=pl.BlockSpec((1, tm, D), lambda b, i: (b, i, 0)),
        out_shape=jax.ShapeDtypeStruct((B, S, D), F32),
        compiler_params=pltpu.CompilerParams(
            dimension_semantics=("arbitrary", "arbitrary"), vmem_limit_bytes=VMEM_LIMIT),
        name="conv_ffn",
    )(x, x, x, mod_l, *weights)


def _final_norm_kernel(x_ref, g_ref, o_ref):
    o_ref[0] = _rms(x_ref[0]) * g_ref[...]


def _final_norm(x, gain, tm):
    B, S, D = x.shape
    tok = pl.BlockSpec((1, tm, D), lambda b, i: (b, i, 0))
    return pl.pallas_call(
        _final_norm_kernel,
        grid=(B, S // tm),
        in_specs=[tok, pl.BlockSpec((1, D), lambda b, i: (0, 0))],
        out_specs=tok,
        out_shape=jax.ShapeDtypeStruct((B, S, D), F32),
        compiler_params=pltpu.CompilerParams(
            dimension_semantics=("arbitrary", "arbitrary"), vmem_limit_bytes=VMEM_LIMIT),
        name="final_norm",
    )(x, gain.reshape(1, D))


def _block_diag2(a, b):
    za = jnp.zeros((a.shape[0], b.shape[1]), a.dtype)
    zb = jnp.zeros((b.shape[0], a.shape[1]), a.dtype)
    return jnp.concatenate([jnp.concatenate([a, za], axis=1),
                            jnp.concatenate([zb, b], axis=1)], axis=0)


def _swap_halves(cols):
    half = cols.shape[-1] // 2
    return jnp.concatenate([cols[..., half:], cols[..., :half]], axis=-1)


def _layer_weights(l, p):
    W = RWKV_WIDTH
    w_in = p["w_in"][l]
    mla = w_in[:, :MLA_IN]
    rw = w_in[:, MLA_IN:]
    kpe = mla[:, MLA_LATENT:MLA_IN]

    def lora_last(a):
        return jnp.concatenate(
            [a[:, :3 * W], a[:, 3 * W + LORA_COLS:], a[:, 3 * W:3 * W + LORA_COLS]], axis=1)

    rw_perm = lora_last(rw)
    mu_perm = lora_last(p["rwkv_mu"][l])
    w_in_perm = jnp.concatenate([mla, _swap_halves(kpe), rw_perm], axis=1).astype(BF16)

    w_uq = p["mla_w_uq"][l].reshape(MLA_Q_RANK, MLA_HEADS, MLA_QK)
    pe = w_uq[:, :, MLA_NOPE:]
    w_uq_t = jnp.concatenate([w_uq, _swap_halves(pe)], axis=-1).reshape(
        MLA_Q_RANK, MLA_HEADS * Q_HEAD_ROWS).T.astype(BF16)
    w_ukv = p["mla_w_ukv"][l].reshape(MLA_KV_RANK, MLA_HEADS, MLA_NOPE + MLA_V)
    w_uk = w_ukv[:, :, :MLA_NOPE].reshape(MLA_KV_RANK, MLA_HEADS * MLA_NOPE).astype(BF16)
    w_uv_t = w_ukv[:, :, MLA_NOPE:].reshape(MLA_KV_RANK, MLA_HEADS * MLA_V).T.astype(BF16)

    head = jnp.arange(W) // RWKV_HEAD
    seg = (head[:, None] == head[None, :]).astype(BF16)
    return {
        "w_in": w_in_perm,
        "mu": mu_perm,
        "q_norm": p["mla_q_norm"][l].reshape(1, -1),
        "w_uq_t": w_uq_t,
        "kv_norm": p["mla_kv_norm"][l].reshape(1, -1),
        "w_uk": w_uk,
        "w_uv_t": w_uv_t,
        "out_norm": p["mla_out_norm"][l].reshape(1, -1),
        "w0": p["rwkv_w0"][l].reshape(1, 2 * W),
        "w2": _block_diag2(p["rwkv_w2"][l, 0], p["rwkv_w2"][l, 1]).astype(BF16),
        "a0": p["rwkv_a0"][l].reshape(1, 2 * W),
        "a2": _block_diag2(p["rwkv_a2"][l, 0], p["rwkv_a2"][l, 1]).astype(BF16),
        "g2": p["rwkv_g2"][l].astype(BF16),
        "k_k": p["rwkv_k_k"][l].reshape(1, W),
        "k_a": p["rwkv_k_a"][l].reshape(1, W),
        "r_k": p["rwkv_r_k"][l].reshape(1, W),
        "gn_g": p["rwkv_gn_g"][l].reshape(1, W),
        "gn_b": p["rwkv_gn_b"][l].reshape(1, W),
        "seg": seg,
        "w_out": p["w_out"][l].astype(BF16),
        "w_up": p["ffn_w_up"][l].astype(BF16),
        "conv_w": p["ffn_conv_w"][l],
        "conv_b": p["ffn_conv_b"][l].reshape(1, D_FF),
        "w_down": p["ffn_w_down"][l].astype(BF16),
    }


def _forward(x, c, positions, p, cfg):
    B, S, D = x.shape
    mod = _adaln(c, p["ada_w"], p["ada_b"])
    rope_t, rope = _rope_tables(positions, cfg["tm_in"])
    for l in range(DEPTH):
        lw = _layer_weights(l, p)
        mod_l = mod[l].reshape(B, 1, 6 * D)
        (q_t, k, v_t, r, vv, nkk, w_f, w_b, kd_f, kd_b, b_f, b_b, bonus, g) = _inproj(
            x, mod_l, rope, rope_t, lw, cfg["tm_in"])
        y_f, y_b, o = _mixers(r, w_f, w_b, kd_f, kd_b, vv, b_f, b_b, nkk, q_t, k, v_t,
                              cfg["tb"], cfg["tk"])
        x = _mixout(x, o, y_f, y_b, bonus, g, mod_l, lw, cfg["tm_out"])
        x = _ffn(x, mod_l, lw, cfg["tm_ffn"], cfg["fc"])
    return _final_norm(x, p["final_norm"], cfg["tm_out"])


_CFG = {"tm_in": 256, "tk": 512, "tb": 64, "tm_out": 512, "tm_ffn": 1024, "fc": 256}


def kernel(x, c, positions, ada_w, ada_b, w_in, mla_q_norm, mla_w_uq, mla_kv_norm, mla_w_ukv, mla_out_norm, rwkv_mu, rwkv_w0, rwkv_w2, rwkv_a0, rwkv_a2, rwkv_g2, rwkv_k_k, rwkv_k_a, rwkv_r_k, rwkv_gn_g, rwkv_gn_b, w_out, ffn_w_up, ffn_conv_w, ffn_conv_b, ffn_w_down, final_norm):
    p = dict(ada_w=ada_w, ada_b=ada_b, w_in=w_in, mla_q_norm=mla_q_norm, mla_w_uq=mla_w_uq,
             mla_kv_norm=mla_kv_norm, mla_w_ukv=mla_w_ukv, mla_out_norm=mla_out_norm,
             rwkv_mu=rwkv_mu, rwkv_w0=rwkv_w0, rwkv_w2=rwkv_w2, rwkv_a0=rwkv_a0, rwkv_a2=rwkv_a2,
             rwkv_g2=rwkv_g2, rwkv_k_k=rwkv_k_k, rwkv_k_a=rwkv_k_a, rwkv_r_k=rwkv_r_k,
             rwkv_gn_g=rwkv_gn_g, rwkv_gn_b=rwkv_gn_b, w_out=w_out, ffn_w_up=ffn_w_up,
             ffn_conv_w=ffn_conv_w, ffn_conv_b=ffn_conv_b, ffn_w_down=ffn_w_down,
             final_norm=final_norm)
    return _forward(x, c, positions, p, _CFG)
```

```python
import functools

import jax
import jax.numpy as jnp
from jax import lax
from jax.experimental import pallas as pl
from jax.experimental.pallas import tpu as pltpu

D_MODEL = 1024
DEPTH = 4
MLA_HEADS = 4
MLA_NOPE = 128
MLA_ROPE = 64
MLA_V = 128
MLA_Q_RANK = 256
MLA_KV_RANK = 128
MLA_WIDTH = MLA_HEADS * MLA_V
MLA_QK = MLA_NOPE + MLA_ROPE
ROPE_BASE = 10000.0
RWKV_WIDTH = D_MODEL - MLA_WIDTH
RWKV_HEAD = 64
RWKV_HEADS = RWKV_WIDTH // RWKV_HEAD
DECAY_LORA = 64
AAA_LORA = 64
GATE_LORA = 128
GN_EPS = 64e-5
D_FF = 2816
EPS = 1e-6

MLA_LATENT = MLA_Q_RANK + MLA_KV_RANK
MLA_IN = MLA_LATENT + MLA_ROPE
MLA_COLS = MLA_LATENT + 2 * MLA_ROPE
LORA_COLS = 2 * DECAY_LORA + 2 * AAA_LORA
RWKV_COLS = 3 * RWKV_WIDTH + GATE_LORA + 2 * DECAY_LORA + 2 * AAA_LORA
IN_COLS = MLA_COLS + RWKV_COLS
Q_HEAD_ROWS = MLA_NOPE + 2 * MLA_ROPE
BF16_ROWS = 16
V_ROWS = MLA_V + BF16_ROWS

LANES = 128
HALO = 8
VMEM_LIMIT = 56 * 1024 * 1024
LOG2E = 1.4426950408889634
DECAY_SCALE = 0.6065306597126334

F32 = jnp.float32
BF16 = jnp.bfloat16
NT_DIMS = (((1,), (1,)), ((), ()))


def _sigmoid(x):
    return 1.0 / (1.0 + jnp.exp(-x))


def _rms(x, eps=EPS):
    return x * lax.rsqrt(jnp.mean(x * x, axis=-1, keepdims=True) + eps)


def _seg_dot(x, seg):
    hi = x.astype(BF16)
    lo = (x - hi.astype(F32)).astype(BF16)
    return (jnp.dot(hi, seg, preferred_element_type=F32)
            + jnp.dot(lo, seg, preferred_element_type=F32))


def _halo_bounds(i, nt, tm):
    lo = jnp.where(i > 0, 0, HALO)
    hi = jnp.where(i < nt - 1, tm + 2 * HALO, tm + HALO)
    return lo, hi


def _adaln_kernel(c_ref, w_ref, b_ref, o_ref):
    c = c_ref[...]
    ca = c * _sigmoid(c)
    o_ref[0] = jnp.dot(ca, w_ref[0], precision=lax.Precision.HIGHEST,
                       preferred_element_type=F32) + b_ref[0]


def _adaln(c, ada_w, ada_b):
    L, D, D6 = ada_w.shape
    B = c.shape[0]
    nj = D6 // D
    return pl.pallas_call(
        _adaln_kernel,
        grid=(L, nj),
        in_specs=[
            pl.BlockSpec((B, D), lambda l, j: (0, 0)),
            pl.BlockSpec((1, D, D), lambda l, j: (l, 0, j)),
            pl.BlockSpec((1, 1, D), lambda l, j: (l, 0, j)),
        ],
        out_specs=pl.BlockSpec((1, B, D), lambda l, j: (l, 0, j)),
        out_shape=jax.ShapeDtypeStruct((L, B, D6), F32),
        compiler_params=pltpu.CompilerParams(
            dimension_semantics=("arbitrary", "arbitrary"), vmem_limit_bytes=VMEM_LIMIT),
        name="adaln",
    )(c, ada_w, ada_b.reshape(L, 1, D6))


def _rope_kernel(pos_ref, freq_ref, tab_t_ref, tab_ref):
    ang = freq_ref[...] * pos_ref[0].astype(F32)
    row = lax.broadcasted_iota(jnp.int32, ang.shape, 0)
    cos = jnp.cos(ang)
    sin = jnp.sin(ang)
    half = MLA_ROPE // 2
    tab_t = jnp.where(row < MLA_ROPE, cos, jnp.where(row < MLA_ROPE + half, -sin, sin))
    tab_t_ref[0] = tab_t
    tab_ref[0] = tab_t.T


def _rope_tables(positions, ts):
    B, S = positions.shape
    half = MLA_ROPE // 2
    inv_freq = ROPE_BASE ** (-jnp.arange(half, dtype=F32) / half)
    freq = jnp.tile(inv_freq, LANES // half).reshape(LANES, 1)
    return pl.pallas_call(
        _rope_kernel,
        grid=(B, S // ts),
        in_specs=[
            pl.BlockSpec((1, 1, ts), lambda b, i: (b, 0, i)),
            pl.BlockSpec((LANES, 1), lambda b, i: (0, 0)),
        ],
        out_specs=(pl.BlockSpec((1, LANES, ts), lambda b, i: (b, 0, i)),
                   pl.BlockSpec((1, ts, LANES), lambda b, i: (b, i, 0))),
        out_shape=(jax.ShapeDtypeStruct((B, LANES, S), F32),
                   jax.ShapeDtypeStruct((B, S, LANES), F32)),
        compiler_params=pltpu.CompilerParams(
            dimension_semantics=("arbitrary", "arbitrary"), vmem_limit_bytes=VMEM_LIMIT),
        name="rope_tables",
    )(positions.reshape(B, 1, S), freq)


def _inproj_kernel(xm_ref, xp_ref, xn_ref, mod_ref, rope_ref, rope_t_ref, w_in_ref, mu_ref,
                   qn_ref, wuq_t_ref, kvn_ref, wuk_ref, wuv_t_ref,
                   w0_ref, w2_ref, a0_ref, a2_ref, g2_ref, kk_ref, ka_ref, rk_ref, seg_ref,
                   qt_out, k_out, vt_out, r_out, vv_out, nkk_out, wf_out, wb_out,
                   kdf_out, kdb_out, bf_out, bb_out, bonus_out, g_out, *, tm, nt):
    i = pl.program_id(1)
    D = D_MODEL
    W = RWKV_WIDTH
    te = tm + 2 * HALO
    x_ext = jnp.concatenate([xp_ref[0], xm_ref[0], xn_ref[0]], axis=0)
    mod = mod_ref[0]
    sh1 = mod[:, 0:D]
    sc1 = mod[:, D:2 * D]
    h = _rms(x_ext) * (1.0 + sc1) + sh1
    z = jnp.dot(h.astype(BF16), w_in_ref[...], preferred_element_type=F32)

    zr = z[:, MLA_COLS:]
    row = lax.broadcasted_iota(jnp.int32, (te, 1), 0)
    lo, hi = _halo_bounds(i, nt, tm)
    zr = jnp.where((row >= lo) & (row < hi), zr, 0.0)
    zc = zr[HALO:HALO + tm]
    zp = pltpu.roll(zr, 1, 0)[HALO:HALO + tm]
    zn = pltpu.roll(zr, te - 1, 0)[HALO:HALO + tm]
    mu = mu_ref[...]
    zs = zc + mu[0:1] * (zp - zc) + mu[1:2] * (zn - zc)

    zm = z[HALO:HALO + tm, 0:MLA_COLS]
    hq = (_rms(zm[:, 0:MLA_Q_RANK]) * qn_ref[...]).astype(BF16)
    q_t = lax.dot_general(wuq_t_ref[...], hq, NT_DIMS, preferred_element_type=F32)
    hkv = (_rms(zm[:, MLA_Q_RANK:MLA_Q_RANK + MLA_KV_RANK]) * kvn_ref[...]).astype(BF16)
    kn = jnp.dot(hkv, wuk_ref[...], preferred_element_type=F32)
    v_t = lax.dot_general(wuv_t_ref[...], hkv, NT_DIMS, preferred_element_type=F32)
    kprod = zm[:, MLA_LATENT:MLA_COLS] * rope_ref[0]
    kpe = (kprod + pltpu.roll(kprod, MLA_ROPE, 1))[:, 0:MLA_ROPE].astype(BF16)
    rope_t = rope_t_ref[0]
    qscale = (MLA_QK ** -0.5) * LOG2E
    for hd in range(MLA_HEADS):
        qh = q_t[hd * Q_HEAD_ROWS:(hd + 1) * Q_HEAD_ROWS]
        qprod = qh[MLA_NOPE:] * rope_t
        qpe = qprod[0:MLA_ROPE] + qprod[MLA_ROPE:]
        qt_out[0, hd, 0:MLA_NOPE, :] = (qh[0:MLA_NOPE] * qscale).astype(BF16)
        qt_out[0, hd, MLA_NOPE:MLA_QK, :] = (qpe * qscale).astype(BF16)
        k_out[0, hd, :, 0:MLA_NOPE] = kn[:, hd * MLA_NOPE:(hd + 1) * MLA_NOPE].astype(BF16)
        k_out[0, hd, :, MLA_NOPE:MLA_QK] = kpe
        vt_out[0, hd, 0:MLA_V, :] = v_t[hd * MLA_V:(hd + 1) * MLA_V].astype(BF16)
        vt_out[0, hd, MLA_V:V_ROWS, :] = jnp.ones((BF16_ROWS, tm), BF16)

    r = zs[:, 0:W]
    k = zs[:, W:2 * W]
    v = zs[:, 2 * W:3 * W]
    xg = zs[:, 3 * W:3 * W + GATE_LORA]
    xw = zs[:, 3 * W + GATE_LORA:3 * W + GATE_LORA + 2 * DECAY_LORA]
    xa = zs[:, 3 * W + GATE_LORA + 2 * DECAY_LORA:]
    seg = seg_ref[...]
    g_out[0] = jnp.dot(_sigmoid(xg).astype(BF16), g2_ref[...], preferred_element_type=F32)
    kk = k * kk_ref[...]
    kkn = kk * lax.rsqrt(_seg_dot(kk * kk, seg) + 1e-12)
    lw = jnp.dot(jnp.tanh(xw).astype(BF16), w2_ref[...], preferred_element_type=F32) + w0_ref[...]
    w = jnp.exp(_sigmoid(lw) * (-DECAY_SCALE))
    a = _sigmoid(jnp.dot(xa.astype(BF16), a2_ref[...], preferred_element_type=F32) + a0_ref[...])
    ka = ka_ref[...]
    kd_f = k * (1.0 + (a[:, 0:W] - 1.0) * ka)
    kd_b = k * (1.0 + (a[:, W:] - 1.0) * ka)
    r_out[0] = r
    vv_out[0] = v
    nkk_out[0] = -kkn
    wf_out[0] = w[:, 0:W]
    wb_out[0] = w[:, W:]
    kdf_out[0] = kd_f
    kdb_out[0] = kd_b
    bf_out[0] = kkn * a[:, 0:W]
    bb_out[0] = kkn * a[:, W:]
    bonus_out[0] = _seg_dot(r * (0.5 * (kd_f + kd_b)) * rk_ref[...], seg) * v


def _inproj(x, mod_l, rope, rope_t, lw, tm):
    B, S, D = x.shape
    nt = S // tm
    hb = tm // HALO
    nhb = S // HALO
    W = RWKV_WIDTH
    H = MLA_HEADS

    def full(a):
        return pl.BlockSpec(a.shape, lambda b, i: (0,) * a.ndim)

    weights = [lw["w_in"], lw["mu"], lw["q_norm"], lw["w_uq_t"], lw["kv_norm"], lw["w_uk"],
               lw["w_uv_t"], lw["w0"], lw["w2"], lw["a0"], lw["a2"], lw["g2"], lw["k_k"],
               lw["k_a"], lw["r_k"], lw["seg"]]
    tok = pl.BlockSpec((1, tm, W), lambda b, i: (b, i, 0))
    tok_shape = jax.ShapeDtypeStruct((B, S, W), F32)
    out_shape = (
        jax.ShapeDtypeStruct((B, H, MLA_QK, S), BF16),
        jax.ShapeDtypeStruct((B, H, S, MLA_QK), BF16),
        jax.ShapeDtypeStruct((B, H, V_ROWS, S), BF16),
    ) + (tok_shape,) * 11
    out_specs = (
        pl.BlockSpec((1, H, MLA_QK, tm), lambda b, i: (b, 0, 0, i)),
        pl.BlockSpec((1, H, tm, MLA_QK), lambda b, i: (b, 0, i, 0)),
        pl.BlockSpec((1, H, V_ROWS, tm), lambda b, i: (b, 0, 0, i)),
    ) + (tok,) * 11
    return pl.pallas_call(
        functools.partial(_inproj_kernel, tm=tm, nt=nt),
        grid=(B, nt),
        in_specs=[
            pl.BlockSpec((1, tm, D), lambda b, i: (b, i, 0)),
            pl.BlockSpec((1, HALO, D), lambda b, i: (b, jnp.maximum(i * hb - 1, 0), 0)),
            pl.BlockSpec((1, HALO, D), lambda b, i: (b, jnp.minimum((i + 1) * hb, nhb - 1), 0)),
            pl.BlockSpec((1, 1, 6 * D), lambda b, i: (b, 0, 0)),
            pl.BlockSpec((1, tm, LANES), lambda b, i: (b, i, 0)),
            pl.BlockSpec((1, LANES, tm), lambda b, i: (b, 0, i)),
        ] + [full(a) for a in weights],
        out_specs=out_specs,
        out_shape=out_shape,
        compiler_params=pltpu.CompilerParams(
            dimension_semantics=("arbitrary", "arbitrary"), vmem_limit_bytes=VMEM_LIMIT),
        name="inproj",
    )(x, x, x, mod_l, rope, rope_t, *weights)


SCAN_RING = 4
_R, _W, _KD, _V, _B, _NK = range(6)


def _mixers_kernel(rf_ref, rb_ref, wf_ref, wb_ref, kdf_ref, kdb_ref, vf_ref, vb_ref, bf_ref, bb_ref,
                   nkf_ref, nkb_ref, qt_ref, k_ref, vt_ref, yf_ref, yb_ref, o_ref, *scratch,
                   tb, tq, tk, nk):
    N = RWKV_HEAD
    HP = RWKV_HEADS // 2
    B = rf_ref.shape[0]
    G = HP * B
    slabs = scratch[0:SCAN_RING]
    y_ring = scratch[SCAN_RING:2 * SCAN_RING]
    state_ref, sa_ref, s_ref, cmax_ref, acc_ref, m_ref = scratch[2 * SCAN_RING:]
    C = sa_ref.shape[-1]
    pairs = ((rf_ref, rb_ref), (wf_ref, wb_ref), (kdf_ref, kdb_ref), (vf_ref, vb_ref),
             (bf_ref, bb_ref), (nkf_ref, nkb_ref))

    def fill(slot, j):
        for idx, (f_ref, b_ref) in enumerate(pairs):
            f_row = f_ref[:, j, :]
            b_row = b_ref[:, tb - 1 - j, :]
            rows = ([f_row[:, hp * LANES:(hp + 1) * LANES] for hp in range(HP)]
                    + [b_row[:, hp * LANES:(hp + 1) * LANES] for hp in range(HP)])
            t = jnp.concatenate(rows, axis=0).T
            slabs[slot][idx] = jnp.concatenate([t[0:N], t[N:2 * N]], axis=1)

    def scores(grp, c, slot):
        q_t = qt_ref[0, 0, :, pl.ds(pl.multiple_of(grp * tq, tq), tq)]
        start = pl.multiple_of(c * tk, tk)
        s_t = jnp.dot(k_ref[0, 0, pl.ds(start, tk), :], q_t,
                      preferred_element_type=F32)
        s_ref[slot] = s_t
        cmax_ref[slot] = jnp.max(s_t, axis=0, keepdims=True)

    def consume(c, slot):
        start = pl.multiple_of(c * tk, tk)
        s_t = s_ref[slot]
        m = m_ref[...]
        m_new = jnp.maximum(m, cmax_ref[slot])
        p_t = jnp.exp2(s_t - m_new).astype(BF16)
        alpha = jnp.exp2(m - m_new)
        acc_ref[...] = acc_ref[...] * alpha + jnp.dot(
            vt_ref[0, 0, :, pl.ds(start, tk)], p_t, preferred_element_type=F32)
        m_ref[...] = m_new

    def emit(slot, j):
        y = y_ring[slot][...]
        t = jnp.concatenate([y[:, 0:2 * G], y[:, 2 * G:C]], axis=0).T
        yf_ref[:, j, :] = jnp.concatenate([t[hp * B:(hp + 1) * B] for hp in range(HP)], axis=1)
        yb_ref[:, tb - 1 - j, :] = jnp.concatenate(
            [t[G + hp * B:G + (hp + 1) * B] for hp in range(HP)], axis=1)

    @pl.when(pl.program_id(0) == 0)
    def _():
        state_ref[...] = jnp.zeros_like(state_ref)
        for y_ref in y_ring:
            y_ref[...] = jnp.zeros_like(y_ref)

    fill(0, 0)
    fill(1, 1)
    sa0 = jnp.zeros((N, C), F32)
    for kc in range(N):
        sa0 = sa0 + state_ref[kc] * slabs[0][_NK, pl.ds(kc, 1), :]
    sa_ref[...] = sa0

    def step(grp, c, cur):
        j = grp * nk + c
        nxt = (cur + 1) % SCAN_RING
        fill((cur + 2) % SCAN_RING, jnp.minimum(j + 2, tb - 1))
        emit((cur + SCAN_RING - 1) % SCAN_RING, jnp.maximum(j - 1, 0))
        scores(grp, jnp.minimum(c + 1, nk - 1), (cur + 1) % 2)
        consume(c, cur % 2)
        for hv in range(4):
            rows = slice(hv * (N // 4), (hv + 1) * (N // 4))
            sa = sa_ref[rows, :]
            vv = slabs[cur][_V, rows, :]
            y = jnp.zeros_like(sa)
            sa_next = jnp.zeros_like(sa)
            for kc in range(N):
                row = pl.ds(kc, 1)
                s_new = (state_ref[kc, rows, :] * slabs[cur][_W, row, :]
                         + sa * slabs[cur][_B, row, :] + vv * slabs[cur][_KD, row, :])
                state_ref[kc, rows, :] = s_new
                y = y + s_new * slabs[cur][_R, row, :]
                sa_next = sa_next + s_new * slabs[nxt][_NK, row, :]
            sa_ref[rows, :] = sa_next
            y_ring[cur][rows, :] = y

    def group(grp, _):
        m_ref[...] = jnp.full(m_ref.shape, -jnp.inf, F32)
        acc_ref[...] = jnp.zeros_like(acc_ref)
        scores(grp, 0, 0)

        def body(i, _):
            for u in range(SCAN_RING):
                pl.when(i >= -u)(functools.partial(step, grp, i * SCAN_RING + u, u))
            return 0

        lax.fori_loop(0, nk // SCAN_RING, body, 0)
        acc = acc_ref[...]
        o_ref[0, pl.ds(pl.multiple_of(grp * tq, tq), tq), :] = (
            acc[0:MLA_V] / acc[MLA_V:MLA_V + 1]).T
        return 0

    lax.fori_loop(0, tb // nk, group, 0)
    emit((tb - 1) % SCAN_RING, tb - 1)


def _mixers(r, w_f, w_b, kd_f, kd_b, v, b_f, b_b, nkk, q_t, k, v_t, tb, tk):
    B, S, W = r.shape
    H = MLA_HEADS
    nb = S // tb
    nk = S // tk
    gps = tb // nk
    tq = B * H * tb // gps
    qblk = S // (gps * tq)
    assert tb % nk == 0 and nk % SCAN_RING == 0 and S % (gps * tq) == 0 and nb == B * H * qblk
    chains = 2 * RWKV_HEADS * B
    fwd = pl.BlockSpec((B, tb, W), lambda i: (0, i, 0))
    bwd = pl.BlockSpec((B, tb, W), lambda i: (0, nb - 1 - i, 0))

    def bh(i):
        return i // (H * qblk), (i // qblk) % H

    return pl.pallas_call(
        functools.partial(_mixers_kernel, tb=tb, tq=tq, tk=tk, nk=nk),
        grid=(nb,),
        in_specs=[fwd, bwd] * 6 + [
            pl.BlockSpec((1, 1, MLA_QK, gps * tq), lambda i: (*bh(i), 0, i % qblk)),
            pl.BlockSpec((1, 1, S, MLA_QK), lambda i: (*bh(i), 0, 0)),
            pl.BlockSpec((1, 1, V_ROWS, S), lambda i: (*bh(i), 0, 0)),
        ],
        out_specs=(fwd, bwd,
                   pl.BlockSpec((1, gps * tq, MLA_V), lambda i: (bh(i)[0], i % qblk, bh(i)[1]))),
        out_shape=(jax.ShapeDtypeStruct((B, S, W), F32),) * 2
        + (jax.ShapeDtypeStruct((B, S, H * MLA_V), F32),),
        scratch_shapes=[pltpu.VMEM((6, RWKV_HEAD, chains), F32)] * SCAN_RING
        + [pltpu.VMEM((RWKV_HEAD, chains), F32)] * SCAN_RING
        + [pltpu.VMEM((RWKV_HEAD, RWKV_HEAD, chains), F32),
           pltpu.VMEM((RWKV_HEAD, chains), F32),
           pltpu.VMEM((2, tk, tq), F32), pltpu.VMEM((2, 1, tq), F32),
           pltpu.VMEM((V_ROWS, tq), F32), pltpu.VMEM((1, tq), F32)],
        compiler_params=pltpu.CompilerParams(
            dimension_semantics=("arbitrary",), vmem_limit_bytes=VMEM_LIMIT),
        name="token_mixers",
    )(r, r, w_f, w_b, kd_f, kd_b, v, v, b_f, b_b, nkk, nkk, q_t, k, v_t)


def _mixout_kernel(x_ref, o_ref, yf_ref, yb_ref, bonus_ref, g_ref, mod_ref, on_ref, gng_ref,
                   gnb_ref, seg_ref, wo_ref, out_ref):
    D = D_MODEL
    gt1 = mod_ref[0][:, 2 * D:3 * D]
    seg = seg_ref[...]
    y_mla = _rms(o_ref[0]) * on_ref[...]
    y = yf_ref[0] + yb_ref[0]
    inv_n = 1.0 / RWKV_HEAD
    d = y - _seg_dot(y, seg) * inv_n
    var = _seg_dot(d * d, seg) * inv_n
    yn = d * lax.rsqrt(var + GN_EPS) * gng_ref[...] + gnb_ref[...]
    y_rwkv = (yn + bonus_ref[0]) * g_ref[0]
    proj = (jnp.dot(y_mla.astype(BF16), wo_ref[0:MLA_WIDTH, :], preferred_element_type=F32)
            + jnp.dot(y_rwkv.astype(BF16), wo_ref[MLA_WIDTH:, :], preferred_element_type=F32))
    out_ref[0] = x_ref[0] + gt1 * proj


def _mixout(x, o, y_f, y_b, bonus, g, mod_l, lw, tm):
    B, S, D = x.shape
    W = RWKV_WIDTH

    def full(a):
        return pl.BlockSpec(a.shape, lambda b, i: (0,) * a.ndim)

    weights = [lw["out_norm"], lw["gn_g"], lw["gn_b"], lw["seg"], lw["w_out"]]
    tokw = pl.BlockSpec((1, tm, W), lambda b, i: (b, i, 0))
    tokd = pl.BlockSpec((1, tm, D), lambda b, i: (b, i, 0))
    return pl.pallas_call(
        _mixout_kernel,
        grid=(B, S // tm),
        in_specs=[tokd, tokw, tokw, tokw, tokw, tokw,
                  pl.BlockSpec((1, 1, 6 * D), lambda b, i: (b, 0, 0))] + [full(a) for a in weights],
        out_specs=tokd,
        out_shape=jax.ShapeDtypeStruct((B, S, D), F32),
        compiler_params=pltpu.CompilerParams(
            dimension_semantics=("arbitrary", "arbitrary"), vmem_limit_bytes=VMEM_LIMIT),
        name="mixer_out",
    )(x, o, y_f, y_b, bonus, g, mod_l, *weights)


def _ffn_kernel(xm_ref, xp_ref, xn_ref, mod_ref, wup_ref, cw_ref, cb_ref, wd_ref, out_ref, *,
                tm, nt, fc):
    i = pl.program_id(1)
    D = D_MODEL
    te = tm + 2 * HALO
    mod = mod_ref[0]
    x_ext = jnp.concatenate([xp_ref[0], xm_ref[0], xn_ref[0]], axis=0)
    sh2 = mod[:, 3 * D:4 * D]
    sc2 = mod[:, 4 * D:5 * D]
    h = (_rms(x_ext) * (1.0 + sc2) + sh2).astype(BF16)
    hm = h[HALO:HALO + tm]
    row = lax.broadcasted_iota(jnp.int32, (te, 1), 0)
    lo, hi = _halo_bounds(i, nt, tm)
    inside = (row >= lo) & (row < hi)
    acc = jnp.zeros((tm, D), F32)
    for s in range(D_FF // fc):
        cols = slice(s * fc, (s + 1) * fc)
        gate = jnp.dot(h, wup_ref[:, cols], preferred_element_type=F32)
        val = jnp.dot(hm, wup_ref[:, D_FF + s * fc:D_FF + (s + 1) * fc],
                      preferred_element_type=F32)
        gate = jnp.where(inside, gate, 0.0)
        gp = pltpu.roll(gate, 1, 0)[HALO:HALO + tm]
        gn = pltpu.roll(gate, te - 1, 0)[HALO:HALO + tm]
        gc = gate[HALO:HALO + tm]
        gg = (cw_ref[0:1, cols] * gp + cw_ref[1:2, cols] * gc + cw_ref[2:3, cols] * gn
              + cb_ref[:, cols])
        act = gg * _sigmoid(gg) * val
        acc = acc + jnp.dot(act.astype(BF16), wd_ref[cols, :], preferred_element_type=F32)
    gt2 = mod[:, 5 * D:6 * D]
    out_ref[0] = xm_ref[0] + gt2 * acc


def _ffn(x, mod_l, lw, tm, fc):
    B, S, D = x.shape
    nt = S // tm
    hb = tm // HALO
    nhb = S // HALO

    def resident(a):
        return pl.BlockSpec(a.shape, lambda b, i: (0,) * a.ndim, pipeline_mode=pl.Buffered(1))

    weights = [lw["w_up"], lw["conv_w"], lw["conv_b"], lw["w_down"]]
    return pl.pallas_call(
        functools.partial(_ffn_kernel, tm=tm, nt=nt, fc=fc),
        grid=(B, nt),
        in_specs=[
            pl.BlockSpec((1, tm, D), lambda b, i: (b, i, 0)),
            pl.BlockSpec((1, HALO, D), lambda b, i: (b, jnp.maximum(i * hb - 1, 0), 0)),
            pl.BlockSpec((1, HALO, D), lambda b, i: (b, jnp.minimum((i + 1) * hb, nhb - 1), 0)),
            pl.BlockSpec((1, 1, 6 * D), lambda b, i: (b, 0, 0)),
        ] + [resident(a) for a in weights],
        out_specs=pl.BlockSpec((1, tm, D), lambda b, i: (b, i, 0)),
        out_shape=jax.ShapeDtypeStruct((B, S, D), F32),
        compiler_params=pltpu.CompilerParams(
            dimension_semantics=("arbitrary", "arbitrary"), vmem_limit_bytes=VMEM_LIMIT),
        name="conv_ffn",
    )(x, x, x, mod_l, *weights)


def _final_norm_kernel(x_ref, g_ref, o_ref):
    o_ref[0] = _rms(x_ref[0]) * g_ref[...]


def _final_norm(x, gain, tm):
    B, S, D = x.shape
    tok = pl.BlockSpec((1, tm, D), lambda b, i: (b, i, 0))
    return pl.pallas_call(
        _final_norm_kernel,
        grid=(B, S // tm),
        in_specs=[tok, pl.BlockSpec((1, D), lambda b, i: (0, 0))],
        out_specs=tok,
        out_shape=jax.ShapeDtypeStruct((B, S, D), F32),
        compiler_params=pltpu.CompilerParams(
            dimension_semantics=("arbitrary", "arbitrary"), vmem_limit_bytes=VMEM_LIMIT),
        name="final_norm",
    )(x, gain.reshape(1, D))


def _block_diag2(a, b):
    za = jnp.zeros((a.shape[0], b.shape[1]), a.dtype)
    zb = jnp.zeros((b.shape[0], a.shape[1]), a.dtype)
    return jnp.concatenate([jnp.concatenate([a, za], axis=1),
                            jnp.concatenate([zb, b], axis=1)], axis=0)


def _swap_halves(cols):
    half = cols.shape[-1] // 2
    return jnp.concatenate([cols[..., half:], cols[..., :half]], axis=-1)


def _layer_weights(l, p):
    W = RWKV_WIDTH
    w_in = p["w_in"][l]
    mla = w_in[:, :MLA_IN]
    rw = w_in[:, MLA_IN:]
    kpe = mla[:, MLA_LATENT:MLA_IN]

    def lora_last(a):
        return jnp.concatenate(
            [a[:, :3 * W], a[:, 3 * W + LORA_COLS:], a[:, 3 * W:3 * W + LORA_COLS]], axis=1)

    rw_perm = lora_last(rw)
    mu_perm = lora_last(p["rwkv_mu"][l])
    w_in_perm = jnp.concatenate([mla, _swap_halves(kpe), rw_perm], axis=1).astype(BF16)

    w_uq = p["mla_w_uq"][l].reshape(MLA_Q_RANK, MLA_HEADS, MLA_QK)
    pe = w_uq[:, :, MLA_NOPE:]
    w_uq_t = jnp.concatenate([w_uq, _swap_halves(pe)], axis=-1).reshape(
        MLA_Q_RANK, MLA_HEADS * Q_HEAD_ROWS).T.astype(BF16)
    w_ukv = p["mla_w_ukv"][l].reshape(MLA_KV_RANK, MLA_HEADS, MLA_NOPE + MLA_V)
    w_uk = w_ukv[:, :, :MLA_NOPE].reshape(MLA_KV_RANK, MLA_HEADS * MLA_NOPE).astype(BF16)
    w_uv_t = w_ukv[:, :, MLA_NOPE:].reshape(MLA_KV_RANK, MLA_HEADS * MLA_V).T.astype(BF16)

    head = jnp.arange(W) // RWKV_HEAD
    seg = (head[:, None] == head[None, :]).astype(BF16)
    return {
        "w_in": w_in_perm,
        "mu": mu_perm,
        "q_norm": p["mla_q_norm"][l].reshape(1, -1),
        "w_uq_t": w_uq_t,
        "kv_norm": p["mla_kv_norm"][l].reshape(1, -1),
        "w_uk": w_uk,
        "w_uv_t": w_uv_t,
        "out_norm": p["mla_out_norm"][l].reshape(1, -1),
        "w0": p["rwkv_w0"][l].reshape(1, 2 * W),
        "w2": _block_diag2(p["rwkv_w2"][l, 0], p["rwkv_w2"][l, 1]).astype(BF16),
        "a0": p["rwkv_a0"][l].reshape(1, 2 * W),
        "a2": _block_diag2(p["rwkv_a2"][l, 0], p["rwkv_a2"][l, 1]).astype(BF16),
        "g2": p["rwkv_g2"][l].astype(BF16),
        "k_k": p["rwkv_k_k"][l].reshape(1, W),
        "k_a": p["rwkv_k_a"][l].reshape(1, W),
        "r_k": p["rwkv_r_k"][l].reshape(1, W),
        "gn_g": p["rwkv_gn_g"][l].reshape(1, W),
        "gn_b": p["rwkv_gn_b"][l].reshape(1, W),
        "seg": seg,
        "w_out": p["w_out"][l].astype(BF16),
        "w_up": p["ffn_w_up"][l].astype(BF16),
        "conv_w": p["ffn_conv_w"][l],
        "conv_b": p["ffn_conv_b"][l].reshape(1, D_FF),
        "w_down": p["ffn_w_down"][l].astype(BF16),
    }


def _forward(x, c, positions, p, cfg):
    B, S, D = x.shape
    mod = _adaln(c, p["ada_w"], p["ada_b"])
    rope_t, rope = _rope_tables(positions, cfg["tm_in"])
    for l in range(DEPTH):
        lw = _layer_weights(l, p)
        mod_l = mod[l].reshape(B, 1, 6 * D)
        (q_t, k, v_t, r, vv, nkk, w_f, w_b, kd_f, kd_b, b_f, b_b, bonus, g) = _inproj(
            x, mod_l, rope, rope_t, lw, cfg["tm_in"])
        y_f, y_b, o = _mixers(r, w_f, w_b, kd_f, kd_b, vv, b_f, b_b, nkk, q_t, k, v_t,
                              cfg["tb"], cfg["tk"])
        x = _mixout(x, o, y_f, y_b, bonus, g, mod_l, lw, cfg["tm_out"])
        x = _ffn(x, mod_l, lw, cfg["tm_ffn"], cfg["fc"])
    return _final_norm(x, p["final_norm"], cfg["tm_out"])


_CFG = {"tm_in": 512, "tk": 512, "tb": 64, "tm_out": 512, "tm_ffn": 1024, "fc": 256}


def kernel(x, c, positions, ada_w, ada_b, w_in, mla_q_norm, mla_w_uq, mla_kv_norm, mla_w_ukv, mla_out_norm, rwkv_mu, rwkv_w0, rwkv_w2, rwkv_a0, rwkv_a2, rwkv_g2, rwkv_k_k, rwkv_k_a, rwkv_r_k, rwkv_gn_g, rwkv_gn_b, w_out, ffn_w_up, ffn_conv_w, ffn_conv_b, ffn_w_down, final_norm):
    p = dict(ada_w=ada_w, ada_b=ada_b, w_in=w_in, mla_q_norm=mla_q_norm, mla_w_uq=mla_w_uq,
             mla_kv_norm=mla_kv_norm, mla_w_ukv=mla_w_ukv, mla_out_norm=mla_out_norm,
             rwkv_mu=rwkv_mu, rwkv_w0=rwkv_w0, rwkv_w2=rwkv_w2, rwkv_a0=rwkv_a0, rwkv_a2=rwkv_a2,
             rwkv_g2=rwkv_g2, rwkv_k_k=rwkv_k_k, rwkv_k_a=rwkv_k_a, rwkv_r_k=rwkv_r_k,
             rwkv_gn_g=rwkv_gn_g, rwkv_gn_b=rwkv_gn_b, w_out=w_out, ffn_w_up=ffn_w_up,
             ffn_conv_w=ffn_conv_w, ffn_conv_b=ffn_conv_b, ffn_w_down=ffn_w_down,
             final_norm=final_norm)
    return _forward(x, c, positions, p, _CFG)
```

```python
import functools

import jax
import jax.numpy as jnp
from jax import lax
from jax.experimental import pallas as pl
from jax.experimental.pallas import tpu as pltpu

D_MODEL = 1024
DEPTH = 4
MLA_HEADS = 4
MLA_NOPE = 128
MLA_ROPE = 64
MLA_V = 128
MLA_Q_RANK = 256
MLA_KV_RANK = 128
MLA_WIDTH = MLA_HEADS * MLA_V
MLA_QK = MLA_NOPE + MLA_ROPE
ROPE_BASE = 10000.0
RWKV_WIDTH = D_MODEL - MLA_WIDTH
RWKV_HEAD = 64
RWKV_HEADS = RWKV_WIDTH // RWKV_HEAD
DECAY_LORA = 64
AAA_LORA = 64
GATE_LORA = 128
GN_EPS = 64e-5
D_FF = 2816
EPS = 1e-6

MLA_LATENT = MLA_Q_RANK + MLA_KV_RANK
MLA_IN = MLA_LATENT + MLA_ROPE
MLA_COLS = MLA_LATENT + 2 * MLA_ROPE
LORA_COLS = 2 * DECAY_LORA + 2 * AAA_LORA
RWKV_COLS = 3 * RWKV_WIDTH + GATE_LORA + 2 * DECAY_LORA + 2 * AAA_LORA
IN_COLS = MLA_COLS + RWKV_COLS
Q_HEAD_ROWS = MLA_NOPE + 2 * MLA_ROPE
BF16_ROWS = 16
V_ROWS = MLA_V + BF16_ROWS

LANES = 128
HALO = 8
VMEM_LIMIT = 56 * 1024 * 1024
LOG2E = 1.4426950408889634
DECAY_SCALE = 0.6065306597126334

F32 = jnp.float32
BF16 = jnp.bfloat16
NT_DIMS = (((1,), (1,)), ((), ()))


def _sigmoid(x):
    return 1.0 / (1.0 + jnp.exp(-x))


def _rms(x, eps=EPS):
    return x * lax.rsqrt(jnp.mean(x * x, axis=-1, keepdims=True) + eps)


def _seg_dot(x, seg):
    hi = x.astype(BF16)
    lo = (x - hi.astype(F32)).astype(BF16)
    return (jnp.dot(hi, seg, preferred_element_type=F32)
            + jnp.dot(lo, seg, preferred_element_type=F32))


def _halo_bounds(i, nt, tm):
    lo = jnp.where(i > 0, 0, HALO)
    hi = jnp.where(i < nt - 1, tm + 2 * HALO, tm + HALO)
    return lo, hi


def _adaln_kernel(c_ref, w_ref, b_ref, o_ref):
    c = c_ref[...]
    ca = c * _sigmoid(c)
    o_ref[0] = jnp.dot(ca, w_ref[0], precision=lax.Precision.HIGHEST,
                       preferred_element_type=F32) + b_ref[0]


def _adaln(c, ada_w, ada_b):
    L, D, D6 = ada_w.shape
    B = c.shape[0]
    nj = D6 // D
    return pl.pallas_call(
        _adaln_kernel,
        grid=(L, nj),
        in_specs=[
            pl.BlockSpec((B, D), lambda l, j: (0, 0)),
            pl.BlockSpec((1, D, D), lambda l, j: (l, 0, j)),
            pl.BlockSpec((1, 1, D), lambda l, j: (l, 0, j)),
        ],
        out_specs=pl.BlockSpec((1, B, D), lambda l, j: (l, 0, j)),
        out_shape=jax.ShapeDtypeStruct((L, B, D6), F32),
        compiler_params=pltpu.CompilerParams(
            dimension_semantics=("arbitrary", "arbitrary"), vmem_limit_bytes=VMEM_LIMIT),
        name="adaln",
    )(c, ada_w, ada_b.reshape(L, 1, D6))


def _rope_kernel(pos_ref, freq_ref, tab_t_ref, tab_ref):
    ang = freq_ref[...] * pos_ref[0].astype(F32)
    row = lax.broadcasted_iota(jnp.int32, ang.shape, 0)
    cos = jnp.cos(ang)
    sin = jnp.sin(ang)
    half = MLA_ROPE // 2
    tab_t = jnp.where(row < MLA_ROPE, cos, jnp.where(row < MLA_ROPE + half, -sin, sin))
    tab_t_ref[0] = tab_t
    tab_ref[0] = tab_t.T


def _rope_tables(positions, ts):
    B, S = positions.shape
    half = MLA_ROPE // 2
    inv_freq = ROPE_BASE ** (-jnp.arange(half, dtype=F32) / half)
    freq = jnp.tile(inv_freq, LANES // half).reshape(LANES, 1)
    return pl.pallas_call(
        _rope_kernel,
        grid=(B, S // ts),
        in_specs=[
            pl.BlockSpec((1, 1, ts), lambda b, i: (b, 0, i)),
            pl.BlockSpec((LANES, 1), lambda b, i: (0, 0)),
        ],
        out_specs=(pl.BlockSpec((1, LANES, ts), lambda b, i: (b, 0, i)),
                   pl.BlockSpec((1, ts, LANES), lambda b, i: (b, i, 0))),
        out_shape=(jax.ShapeDtypeStruct((B, LANES, S), F32),
                   jax.ShapeDtypeStruct((B, S, LANES), F32)),
        compiler_params=pltpu.CompilerParams(
            dimension_semantics=("arbitrary", "arbitrary"), vmem_limit_bytes=VMEM_LIMIT),
        name="rope_tables",
    )(positions.reshape(B, 1, S), freq)


def _inproj_kernel(xm_ref, xp_ref, xn_ref, mod_ref, rope_ref, rope_t_ref, w_in_ref, mu_ref,
                   qn_ref, wuq_t_ref, kvn_ref, wuk_ref, wuv_t_ref,
                   w0_ref, w2_ref, a0_ref, a2_ref, g2_ref, kk_ref, ka_ref, rk_ref, seg_ref,
                   qt_out, k_out, vt_out, r_out, vv_out, nkk_out, wf_out, wb_out,
                   kdf_out, kdb_out, bf_out, bb_out, bonus_out, g_out, *, tm, nt):
    i = pl.program_id(1)
    D = D_MODEL
    W = RWKV_WIDTH
    te = tm + 2 * HALO
    x_ext = jnp.concatenate([xp_ref[0], xm_ref[0], xn_ref[0]], axis=0)
    mod = mod_ref[0]
    sh1 = mod[:, 0:D]
    sc1 = mod[:, D:2 * D]
    h = _rms(x_ext) * (1.0 + sc1) + sh1
    z = jnp.dot(h.astype(BF16), w_in_ref[...], preferred_element_type=F32)

    zr = z[:, MLA_COLS:]
    row = lax.broadcasted_iota(jnp.int32, (te, 1), 0)
    lo, hi = _halo_bounds(i, nt, tm)
    zr = jnp.where((row >= lo) & (row < hi), zr, 0.0)
    zc = zr[HALO:HALO + tm]
    zp = pltpu.roll(zr, 1, 0)[HALO:HALO + tm]
    zn = pltpu.roll(zr, te - 1, 0)[HALO:HALO + tm]
    mu = mu_ref[...]
    zs = zc + mu[0:1] * (zp - zc) + mu[1:2] * (zn - zc)

    zm = z[HALO:HALO + tm, 0:MLA_COLS]
    hq = (_rms(zm[:, 0:MLA_Q_RANK]) * qn_ref[...]).astype(BF16)
    q_t = lax.dot_general(wuq_t_ref[...], hq, NT_DIMS, preferred_element_type=F32)
    hkv = (_rms(zm[:, MLA_Q_RANK:MLA_Q_RANK + MLA_KV_RANK]) * kvn_ref[...]).astype(BF16)
    kn = jnp.dot(hkv, wuk_ref[...], preferred_element_type=F32)
    v_t = lax.dot_general(wuv_t_ref[...], hkv, NT_DIMS, preferred_element_type=F32)
    kprod = zm[:, MLA_LATENT:MLA_COLS] * rope_ref[0]
    kpe = (kprod + pltpu.roll(kprod, MLA_ROPE, 1))[:, 0:MLA_ROPE].astype(BF16)
    rope_t = rope_t_ref[0]
    qscale = (MLA_QK ** -0.5) * LOG2E
    for hd in range(MLA_HEADS):
        qh = q_t[hd * Q_HEAD_ROWS:(hd + 1) * Q_HEAD_ROWS]
        qprod = qh[MLA_NOPE:] * rope_t
        qpe = qprod[0:MLA_ROPE] + qprod[MLA_ROPE:]
        qt_out[0, hd, 0:MLA_NOPE, :] = (qh[0:MLA_NOPE] * qscale).astype(BF16)
        qt_out[0, hd, MLA_NOPE:MLA_QK, :] = (qpe * qscale).astype(BF16)
        k_out[0, hd, :, 0:MLA_NOPE] = kn[:, hd * MLA_NOPE:(hd + 1) * MLA_NOPE].astype(BF16)
        k_out[0, hd, :, MLA_NOPE:MLA_QK] = kpe
        vt_out[0, hd, 0:MLA_V, :] = v_t[hd * MLA_V:(hd + 1) * MLA_V].astype(BF16)
        vt_out[0, hd, MLA_V:V_ROWS, :] = jnp.ones((BF16_ROWS, tm), BF16)

    r = zs[:, 0:W]
    k = zs[:, W:2 * W]
    v = zs[:, 2 * W:3 * W]
    xg = zs[:, 3 * W:3 * W + GATE_LORA]
    xw = zs[:, 3 * W + GATE_LORA:3 * W + GATE_LORA + 2 * DECAY_LORA]
    xa = zs[:, 3 * W + GATE_LORA + 2 * DECAY_LORA:]
    seg = seg_ref[...]
    g_out[0] = jnp.dot(_sigmoid(xg).astype(BF16), g2_ref[...], preferred_element_type=F32)
    kk = k * kk_ref[...]
    kkn = kk * lax.rsqrt(_seg_dot(kk * kk, seg) + 1e-12)
    lw = jnp.dot(jnp.tanh(xw).astype(BF16), w2_ref[...], preferred_element_type=F32) + w0_ref[...]
    w = jnp.exp(_sigmoid(lw) * (-DECAY_SCALE))
    a = _sigmoid(jnp.dot(xa.astype(BF16), a2_ref[...], preferred_element_type=F32) + a0_ref[...])
    ka = ka_ref[...]
    kd_f = k * (1.0 + (a[:, 0:W] - 1.0) * ka)
    kd_b = k * (1.0 + (a[:, W:] - 1.0) * ka)
    r_out[0] = r
    vv_out[0] = v
    nkk_out[0] = -kkn
    wf_out[0] = w[:, 0:W]
    wb_out[0] = w[:, W:]
    kdf_out[0] = kd_f
    kdb_out[0] = kd_b
    bf_out[0] = kkn * a[:, 0:W]
    bb_out[0] = kkn * a[:, W:]
    bonus_out[0] = _seg_dot(r * (0.5 * (kd_f + kd_b)) * rk_ref[...], seg) * v


def _inproj(x, mod_l, rope, rope_t, lw, tm):
    B, S, D = x.shape
    nt = S // tm
    hb = tm // HALO
    nhb = S // HALO
    W = RWKV_WIDTH
    H = MLA_HEADS

    def full(a):
        return pl.BlockSpec(a.shape, lambda b, i: (0,) * a.ndim)

    weights = [lw["w_in"], lw["mu"], lw["q_norm"], lw["w_uq_t"], lw["kv_norm"], lw["w_uk"],
               lw["w_uv_t"], lw["w0"], lw["w2"], lw["a0"], lw["a2"], lw["g2"], lw["k_k"],
               lw["k_a"], lw["r_k"], lw["seg"]]
    tok = pl.BlockSpec((1, tm, W), lambda b, i: (b, i, 0))
    tok_shape = jax.ShapeDtypeStruct((B, S, W), F32)
    out_shape = (
        jax.ShapeDtypeStruct((B, H, MLA_QK, S), BF16),
        jax.ShapeDtypeStruct((B, H, S, MLA_QK), BF16),
        jax.ShapeDtypeStruct((B, H, V_ROWS, S), BF16),
    ) + (tok_shape,) * 11
    out_specs = (
        pl.BlockSpec((1, H, MLA_QK, tm), lambda b, i: (b, 0, 0, i)),
        pl.BlockSpec((1, H, tm, MLA_QK), lambda b, i: (b, 0, i, 0)),
        pl.BlockSpec((1, H, V_ROWS, tm), lambda b, i: (b, 0, 0, i)),
    ) + (tok,) * 11
    return pl.pallas_call(
        functools.partial(_inproj_kernel, tm=tm, nt=nt),
        grid=(B, nt),
        in_specs=[
            pl.BlockSpec((1, tm, D), lambda b, i: (b, i, 0)),
            pl.BlockSpec((1, HALO, D), lambda b, i: (b, jnp.maximum(i * hb - 1, 0), 0)),
            pl.BlockSpec((1, HALO, D), lambda b, i: (b, jnp.minimum((i + 1) * hb, nhb - 1), 0)),
            pl.BlockSpec((1, 1, 6 * D), lambda b, i: (b, 0, 0)),
            pl.BlockSpec((1, tm, LANES), lambda b, i: (b, i, 0)),
            pl.BlockSpec((1, LANES, tm), lambda b, i: (b, 0, i)),
        ] + [full(a) for a in weights],
        out_specs=out_specs,
        out_shape=out_shape,
        compiler_params=pltpu.CompilerParams(
            dimension_semantics=("arbitrary", "arbitrary"), vmem_limit_bytes=VMEM_LIMIT),
        name="inproj",
    )(x, x, x, mod_l, rope, rope_t, *weights)


SCAN_RING = 4
_R, _W, _KD, _V, _B, _NK = range(6)


def _mixers_kernel(rf_ref, rb_ref, wf_ref, wb_ref, kdf_ref, kdb_ref, vf_ref, vb_ref, bf_ref, bb_ref,
                   nkf_ref, nkb_ref, qt_ref, k_ref, vt_ref, yf_ref, yb_ref, o_ref, *scratch,
                   tb, tq, tk, nk):
    N = RWKV_HEAD
    HP = RWKV_HEADS // 2
    B = rf_ref.shape[0]
    G = HP * B
    slabs = scratch[0:SCAN_RING]
    y_ring = scratch[SCAN_RING:2 * SCAN_RING]
    state_ref, sa_ref, s_ref, cmax_ref, acc_ref, m_ref = scratch[2 * SCAN_RING:]
    C = sa_ref.shape[-1]
    pairs = ((rf_ref, rb_ref), (wf_ref, wb_ref), (kdf_ref, kdb_ref), (vf_ref, vb_ref),
             (bf_ref, bb_ref), (nkf_ref, nkb_ref))

    def fill(slot, j):
        for idx, (f_ref, b_ref) in enumerate(pairs):
            f_row = f_ref[:, j, :]
            b_row = b_ref[:, tb - 1 - j, :]
            rows = ([f_row[:, hp * LANES:(hp + 1) * LANES] for hp in range(HP)]
                    + [b_row[:, hp * LANES:(hp + 1) * LANES] for hp in range(HP)])
            t = jnp.concatenate(rows, axis=0).T
            slabs[slot][idx] = jnp.concatenate([t[0:N], t[N:2 * N]], axis=1)

    def scores(grp, c, slot):
        q_t = qt_ref[0, 0, :, pl.ds(pl.multiple_of(grp * tq, tq), tq)]
        start = pl.multiple_of(c * tk, tk)
        s_t = jnp.dot(k_ref[0, 0, pl.ds(start, tk), :], q_t,
                      preferred_element_type=F32)
        s_ref[slot] = s_t
        cmax_ref[slot] = jnp.max(s_t, axis=0, keepdims=True)

    def consume(c, slot):
        start = pl.multiple_of(c * tk, tk)
        s_t = s_ref[slot]
        m = m_ref[...]
        m_new = jnp.maximum(m, cmax_ref[slot])
        p_t = jnp.exp2(s_t - m_new).astype(BF16)
        alpha = jnp.exp2(m - m_new)
        acc_ref[...] = acc_ref[...] * alpha + jnp.dot(
            vt_ref[0, 0, :, pl.ds(start, tk)], p_t, preferred_element_type=F32)
        m_ref[...] = m_new

    def emit(slot, j):
        y = y_ring[slot][...]
        t = jnp.concatenate([y[:, 0:2 * G], y[:, 2 * G:C]], axis=0).T
        yf_ref[:, j, :] = jnp.concatenate([t[hp * B:(hp + 1) * B] for hp in range(HP)], axis=1)
        yb_ref[:, tb - 1 - j, :] = jnp.concatenate(
            [t[G + hp * B:G + (hp + 1) * B] for hp in range(HP)], axis=1)

    @pl.when(pl.program_id(0) == 0)
    def _():
        state_ref[...] = jnp.zeros_like(state_ref)
        for y_ref in y_ring:
            y_ref[...] = jnp.zeros_like(y_ref)

    fill(0, 0)
    fill(1, 1)
    sa0 = jnp.zeros((N, C), F32)
    for kc in range(N):
        sa0 = sa0 + state_ref[kc] * slabs[0][_NK, pl.ds(kc, 1), :]
    sa_ref[...] = sa0

    def step(grp, c, cur):
        j = grp * nk + c
        nxt = (cur + 1) % SCAN_RING
        fill((cur + 2) % SCAN_RING, jnp.minimum(j + 2, tb - 1))
        emit((cur + SCAN_RING - 1) % SCAN_RING, jnp.maximum(j - 1, 0))
        scores(grp, jnp.minimum(c + 1, nk - 1), (cur + 1) % 2)
        consume(c, cur % 2)
        for hv in range(8):
            rows = slice(hv * (N // 8), (hv + 1) * (N // 8))
            sa = sa_ref[rows, :]
            vv = slabs[cur][_V, rows, :]
            y = jnp.zeros_like(sa)
            sa_next = jnp.zeros_like(sa)
            for kc in range(N):
                row = pl.ds(kc, 1)
                s_new = (state_ref[kc, rows, :] * slabs[cur][_W, row, :]
                         + sa * slabs[cur][_B, row, :] + vv * slabs[cur][_KD, row, :])
                state_ref[kc, rows, :] = s_new
                y = y + s_new * slabs[cur][_R, row, :]
                sa_next = sa_next + s_new * slabs[nxt][_NK, row, :]
            sa_ref[rows, :] = sa_next
            y_ring[cur][rows, :] = y

    def group(grp, _):
        m_ref[...] = jnp.full(m_ref.shape, -jnp.inf, F32)
        acc_ref[...] = jnp.zeros_like(acc_ref)
        scores(grp, 0, 0)

        def body(i, _):
            for u in range(SCAN_RING):
                pl.when(i >= -u)(functools.partial(step, grp, i * SCAN_RING + u, u))
            return 0

        lax.fori_loop(0, nk // SCAN_RING, body, 0)
        acc = acc_ref[...]
        o_ref[0, pl.ds(pl.multiple_of(grp * tq, tq), tq), :] = (
            acc[0:MLA_V] / acc[MLA_V:MLA_V + 1]).T
        return 0

    lax.fori_loop(0, tb // nk, group, 0)
    emit((tb - 1) % SCAN_RING, tb - 1)


def _mixers(r, w_f, w_b, kd_f, kd_b, v, b_f, b_b, nkk, q_t, k, v_t, tb, tk):
    B, S, W = r.shape
    H = MLA_HEADS
    nb = S // tb
    nk = S // tk
    gps = tb // nk
    tq = B * H * tb // gps
    qblk = S // (gps * tq)
    assert tb % nk == 0 and nk % SCAN_RING == 0 and S % (gps * tq) == 0 and nb == B * H * qblk
    chains = 2 * RWKV_HEADS * B
    fwd = pl.BlockSpec((B, tb, W), lambda i: (0, i, 0))
    bwd = pl.BlockSpec((B, tb, W), lambda i: (0, nb - 1 - i, 0))

    def bh(i):
        return i // (H * qblk), (i // qblk) % H

    return pl.pallas_call(
        functools.partial(_mixers_kernel, tb=tb, tq=tq, tk=tk, nk=nk),
        grid=(nb,),
        in_specs=[fwd, bwd] * 6 + [
            pl.BlockSpec((1, 1, MLA_QK, gps * tq), lambda i: (*bh(i), 0, i % qblk)),
            pl.BlockSpec((1, 1, S, MLA_QK), lambda i: (*bh(i), 0, 0)),
            pl.BlockSpec((1, 1, V_ROWS, S), lambda i: (*bh(i), 0, 0)),
        ],
        out_specs=(fwd, bwd,
                   pl.BlockSpec((1, gps * tq, MLA_V), lambda i: (bh(i)[0], i % qblk, bh(i)[1]))),
        out_shape=(jax.ShapeDtypeStruct((B, S, W), F32),) * 2
        + (jax.ShapeDtypeStruct((B, S, H * MLA_V), F32),),
        scratch_shapes=[pltpu.VMEM((6, RWKV_HEAD, chains), F32)] * SCAN_RING
        + [pltpu.VMEM((RWKV_HEAD, chains), F32)] * SCAN_RING
        + [pltpu.VMEM((RWKV_HEAD, RWKV_HEAD, chains), F32),
           pltpu.VMEM((RWKV_HEAD, chains), F32),
           pltpu.VMEM((2, tk, tq), F32), pltpu.VMEM((2, 1, tq), F32),
           pltpu.VMEM((V_ROWS, tq), F32), pltpu.VMEM((1, tq), F32)],
        compiler_params=pltpu.CompilerParams(
            dimension_semantics=("arbitrary",), vmem_limit_bytes=VMEM_LIMIT),
        name="token_mixers",
    )(r, r, w_f, w_b, kd_f, kd_b, v, v, b_f, b_b, nkk, nkk, q_t, k, v_t)


def _mixout_kernel(x_ref, o_ref, yf_ref, yb_ref, bonus_ref, g_ref, mod_ref, on_ref, gng_ref,
                   gnb_ref, seg_ref, wo_ref, out_ref):
    D = D_MODEL
    gt1 = mod_ref[0][:, 2 * D:3 * D]
    seg = seg_ref[...]
    y_mla = _rms(o_ref[0]) * on_ref[...]
    y = yf_ref[0] + yb_ref[0]
    inv_n = 1.0 / RWKV_HEAD
    d = y - _seg_dot(y, seg) * inv_n
    var = _seg_dot(d * d, seg) * inv_n
    yn = d * lax.rsqrt(var + GN_EPS) * gng_ref[...] + gnb_ref[...]
    y_rwkv = (yn + bonus_ref[0]) * g_ref[0]
    proj = (jnp.dot(y_mla.astype(BF16), wo_ref[0:MLA_WIDTH, :], preferred_element_type=F32)
            + jnp.dot(y_rwkv.astype(BF16), wo_ref[MLA_WIDTH:, :], preferred_element_type=F32))
    out_ref[0] = x_ref[0] + gt1 * proj


def _mixout(x, o, y_f, y_b, bonus, g, mod_l, lw, tm):
    B, S, D = x.shape
    W = RWKV_WIDTH

    def full(a):
        return pl.BlockSpec(a.shape, lambda b, i: (0,) * a.ndim)

    weights = [lw["out_norm"], lw["gn_g"], lw["gn_b"], lw["seg"], lw["w_out"]]
    tokw = pl.BlockSpec((1, tm, W), lambda b, i: (b, i, 0))
    tokd = pl.BlockSpec((1, tm, D), lambda b, i: (b, i, 0))
    return pl.pallas_call(
        _mixout_kernel,
        grid=(B, S // tm),
        in_specs=[tokd, tokw, tokw, tokw, tokw, tokw,
                  pl.BlockSpec((1, 1, 6 * D), lambda b, i: (b, 0, 0))] + [full(a) for a in weights],
        out_specs=tokd,
        out_shape=jax.ShapeDtypeStruct((B, S, D), F32),
        compiler_params=pltpu.CompilerParams(
            dimension_semantics=("arbitrary", "arbitrary"), vmem_limit_bytes=VMEM_LIMIT),
        name="mixer_out",
    )(x, o, y_f, y_b, bonus, g, mod_l, *weights)


def _ffn_kernel(xm_ref, xp_ref, xn_ref, mod_ref, wup_ref, cw_ref, cb_ref, wd_ref, out_ref, *,
                tm, nt, fc):
    i = pl.program_id(1)
    D = D_MODEL
    te = tm + 2 * HALO
    mod = mod_ref[0]
    x_ext = jnp.concatenate([xp_ref[0], xm_ref[0], xn_ref[0]], axis=0)
    sh2 = mod[:, 3 * D:4 * D]
    sc2 = mod[:, 4 * D:5 * D]
    h = (_rms(x_ext) * (1.0 + sc2) + sh2).astype(BF16)
    hm = h[HALO:HALO + tm]
    row = lax.broadcasted_iota(jnp.int32, (te, 1), 0)
    lo, hi = _halo_bounds(i, nt, tm)
    inside = (row >= lo) & (row < hi)
    acc = jnp.zeros((tm, D), F32)
    for s in range(D_FF // fc):
        cols = slice(s * fc, (s + 1) * fc)
        gate = jnp.dot(h, wup_ref[:, cols], preferred_element_type=F32)
        val = jnp.dot(hm, wup_ref[:, D_FF + s * fc:D_FF + (s + 1) * fc],
                      preferred_element_type=F32)
        gate = jnp.where(inside, gate, 0.0)
        gp = pltpu.roll(gate, 1, 0)[HALO:HALO + tm]
        gn = pltpu.roll(gate, te - 1, 0)[HALO:HALO + tm]
        gc = gate[HALO:HALO + tm]
        gg = (cw_ref[0:1, cols] * gp + cw_ref[1:2, cols] * gc + cw_ref[2:3, cols] * gn
              + cb_ref[:, cols])
        act = gg * _sigmoid(gg) * val
        acc = acc + jnp.dot(act.astype(BF16), wd_ref[cols, :], preferred_element_type=F32)
    gt2 = mod[:, 5 * D:6 * D]
    out_ref[0] = xm_ref[0] + gt2 * acc


def _ffn(x, mod_l, lw, tm, fc):
    B, S, D = x.shape
    nt = S // tm
    hb = tm // HALO
    nhb = S // HALO

    def resident(a):
        return pl.BlockSpec(a.shape, lambda b, i: (0,) * a.ndim, pipeline_mode=pl.Buffered(1))

    weights = [lw["w_up"], lw["conv_w"], lw["conv_b"], lw["w_down"]]
    return pl.pallas_call(
        functools.partial(_ffn_kernel, tm=tm, nt=nt, fc=fc),
        grid=(B, nt),
        in_specs=[
            pl.BlockSpec((1, tm, D), lambda b, i: (b, i, 0)),
            pl.BlockSpec((1, HALO, D), lambda b, i: (b, jnp.maximum(i * hb - 1, 0), 0)),
            pl.BlockSpec((1, HALO, D), lambda b, i: (b, jnp.minimum((i + 1) * hb, nhb - 1), 0)),
            pl.BlockSpec((1, 1, 6 * D), lambda b, i: (b, 0, 0)),
        ] + [resident(a) for a in weights],
        out_specs=pl.BlockSpec((1, tm, D), lambda b, i: (b, i, 0)),
        out_shape=jax.ShapeDtypeStruct((B, S, D), F32),
        compiler_params=pltpu.CompilerParams(
            dimension_semantics=("arbitrary", "arbitrary"), vmem_limit_bytes=VMEM_LIMIT),
        name="conv_ffn",
    )(x, x, x, mod_l, *weights)


def _final_norm_kernel(x_ref, g_ref, o_ref):
    o_ref[0] = _rms(x_ref[0]) * g_ref[...]


def _final_norm(x, gain, tm):
    B, S, D = x.shape
    tok = pl.BlockSpec((1, tm, D), lambda b, i: (b, i, 0))
    return pl.pallas_call(
        _final_norm_kernel,
        grid=(B, S // tm),
        in_specs=[tok, pl.BlockSpec((1, D), lambda b, i: (0, 0))],
        out_specs=tok,
        out_shape=jax.ShapeDtypeStruct((B, S, D), F32),
        compiler_params=pltpu.CompilerParams(
            dimension_semantics=("arbitrary", "arbitrary"), vmem_limit_bytes=VMEM_LIMIT),
        name="final_norm",
    )(x, gain.reshape(1, D))


def _block_diag2(a, b):
    za = jnp.zeros((a.shape[0], b.shape[1]), a.dtype)
    zb = jnp.zeros((b.shape[0], a.shape[1]), a.dtype)
    return jnp.concatenate([jnp.concatenate([a, za], axis=1),
                            jnp.concatenate([zb, b], axis=1)], axis=0)


def _swap_halves(cols):
    half = cols.shape[-1] // 2
    return jnp.concatenate([cols[..., half:], cols[..., :half]], axis=-1)


def _layer_weights(l, p):
    W = RWKV_WIDTH
    w_in = p["w_in"][l]
    mla = w_in[:, :MLA_IN]
    rw = w_in[:, MLA_IN:]
    kpe = mla[:, MLA_LATENT:MLA_IN]

    def lora_last(a):
        return jnp.concatenate(
            [a[:, :3 * W], a[:, 3 * W + LORA_COLS:], a[:, 3 * W:3 * W + LORA_COLS]], axis=1)

    rw_perm = lora_last(rw)
    mu_perm = lora_last(p["rwkv_mu"][l])
    w_in_perm = jnp.concatenate([mla, _swap_halves(kpe), rw_perm], axis=1).astype(BF16)

    w_uq = p["mla_w_uq"][l].reshape(MLA_Q_RANK, MLA_HEADS, MLA_QK)
    pe = w_uq[:, :, MLA_NOPE:]
    w_uq_t = jnp.concatenate([w_uq, _swap_halves(pe)], axis=-1).reshape(
        MLA_Q_RANK, MLA_HEADS * Q_HEAD_ROWS).T.astype(BF16)
    w_ukv = p["mla_w_ukv"][l].reshape(MLA_KV_RANK, MLA_HEADS, MLA_NOPE + MLA_V)
    w_uk = w_ukv[:, :, :MLA_NOPE].reshape(MLA_KV_RANK, MLA_HEADS * MLA_NOPE).astype(BF16)
    w_uv_t = w_ukv[:, :, MLA_NOPE:].reshape(MLA_KV_RANK, MLA_HEADS * MLA_V).T.astype(BF16)

    head = jnp.arange(W) // RWKV_HEAD
    seg = (head[:, None] == head[None, :]).astype(BF16)
    return {
        "w_in": w_in_perm,
        "mu": mu_perm,
        "q_norm": p["mla_q_norm"][l].reshape(1, -1),
        "w_uq_t": w_uq_t,
        "kv_norm": p["mla_kv_norm"][l].reshape(1, -1),
        "w_uk": w_uk,
        "w_uv_t": w_uv_t,
        "out_norm": p["mla_out_norm"][l].reshape(1, -1),
        "w0": p["rwkv_w0"][l].reshape(1, 2 * W),
        "w2": _block_diag2(p["rwkv_w2"][l, 0], p["rwkv_w2"][l, 1]).astype(BF16),
        "a0": p["rwkv_a0"][l].reshape(1, 2 * W),
        "a2": _block_diag2(p["rwkv_a2"][l, 0], p["rwkv_a2"][l, 1]).astype(BF16),
        "g2": p["rwkv_g2"][l].astype(BF16),
        "k_k": p["rwkv_k_k"][l].reshape(1, W),
        "k_a": p["rwkv_k_a"][l].reshape(1, W),
        "r_k": p["rwkv_r_k"][l].reshape(1, W),
        "gn_g": p["rwkv_gn_g"][l].reshape(1, W),
        "gn_b": p["rwkv_gn_b"][l].reshape(1, W),
        "seg": seg,
        "w_out": p["w_out"][l].astype(BF16),
        "w_up": p["ffn_w_up"][l].astype(BF16),
        "conv_w": p["ffn_conv_w"][l],
        "conv_b": p["ffn_conv_b"][l].reshape(1, D_FF),
        "w_down": p["ffn_w_down"][l].astype(BF16),
    }


def _forward(x, c, positions, p, cfg):
    B, S, D = x.shape
    mod = _adaln(c, p["ada_w"], p["ada_b"])
    rope_t, rope = _rope_tables(positions, cfg["tm_in"])
    for l in range(DEPTH):
        lw = _layer_weights(l, p)
        mod_l = mod[l].reshape(B, 1, 6 * D)
        (q_t, k, v_t, r, vv, nkk, w_f, w_b, kd_f, kd_b, b_f, b_b, bonus, g) = _inproj(
            x, mod_l, rope, rope_t, lw, cfg["tm_in"])
        y_f, y_b, o = _mixers(r, w_f, w_b, kd_f, kd_b, vv, b_f, b_b, nkk, q_t, k, v_t,
                              cfg["tb"], cfg["tk"])
        x = _mixout(x, o, y_f, y_b, bonus, g, mod_l, lw, cfg["tm_out"])
        x = _ffn(x, mod_l, lw, cfg["tm_ffn"], cfg["fc"])
    return _final_norm(x, p["final_norm"], cfg["tm_out"])


_CFG = {"tm_in": 512, "tk": 512, "tb": 64, "tm_out": 1024, "tm_ffn": 1024, "fc": 256}


def kernel(x, c, positions, ada_w, ada_b, w_in, mla_q_norm, mla_w_uq, mla_kv_norm, mla_w_ukv, mla_out_norm, rwkv_mu, rwkv_w0, rwkv_w2, rwkv_a0, rwkv_a2, rwkv_g2, rwkv_k_k, rwkv_k_a, rwkv_r_k, rwkv_gn_g, rwkv_gn_b, w_out, ffn_w_up, ffn_conv_w, ffn_conv_b, ffn_w_down, final_norm):
    p = dict(ada_w=ada_w, ada_b=ada_b, w_in=w_in, mla_q_norm=mla_q_norm, mla_w_uq=mla_w_uq,
             mla_kv_norm=mla_kv_norm, mla_w_ukv=mla_w_ukv, mla_out_norm=mla_out_norm,
             rwkv_mu=rwkv_mu, rwkv_w0=rwkv_w0, rwkv_w2=rwkv_w2, rwkv_a0=rwkv_a0, rwkv_a2=rwkv_a2,
             rwkv_g2=rwkv_g2, rwkv_k_k=rwkv_k_k, rwkv_k_a=rwkv_k_a, rwkv_r_k=rwkv_r_k,
             rwkv_gn_g=rwkv_gn_g, rwkv_gn_b=rwkv_gn_b, w_out=w_out, ffn_w_up=ffn_w_up,
             ffn_conv_w=ffn_conv_w, ffn_conv_b=ffn_conv_b, ffn_w_down=ffn_w_down,
             final_norm=final_norm)
    return _forward(x, c, positions, p, _CFG)
```
